```python
import functools
import math
import jax, jax.numpy as jnp
from jax import lax
import numpy as np

D_MODEL = 1024
BATCH = 2
SEQ = 8192
DEPTH = 2
DEC_BATCH = 32
DEC_SEQ = 8
PAST_LEN = 16384
PAGE_SIZE = 128

N_A_LAYERS = (DEPTH + 1) // 2
N_C_LAYERS = DEPTH // 2
EPS = 1e-6
GLA_HEADS = 4
GLA_DV = D_MODEL // 8
GLA_DK = GLA_DV // 2
GLA_GK_RANK = 16
GLA_GATE_NORM = 16.0
GLA_CHUNK = 64
LRU_WIDTH = D_MODEL // 2
LRU_BLOCKS = 8
LRU_BLOCK_W = LRU_WIDTH // LRU_BLOCKS
LRU_CONV_W = 4
LRU_C = 8.0
A_Q = GLA_HEADS * GLA_DK
A_V = GLA_HEADS * GLA_DV
A_SPLITS = (A_Q, 2 * A_Q, 2 * A_Q + A_V, 2 * A_Q + A_V + GLA_GK_RANK, 2 * A_Q + 2 * A_V + GLA_GK_RANK, 2 * A_Q + 2 * A_V + GLA_GK_RANK + LRU_WIDTH)
IN_A = 2 * A_Q + 2 * A_V + GLA_GK_RANK + 2 * LRU_WIDTH
MIX_A = A_V + LRU_WIDTH
FOX_HEADS = 16
FOX_HD = D_MODEL // FOX_HEADS
FOX_BLOCK = 128
FOX_SCALE = FOX_HD ** -0.5
D_ATT = FOX_HEADS * FOX_HD
C_SPLITS = (D_ATT, 2 * D_ATT, 3 * D_ATT)
IN_C = 3 * D_ATT + FOX_HEADS
FOX_BIAS_INIT = 3.0
CACHE_LOGIT_CENTRE = 6.0
D_FF = ((8 * D_MODEL // 3 + 127) // 128) * 128
FFN_CONV_W = 3

kernel_name = 'hybrid_gla_rglru_fox_convffn_step'


def rmsnorm(x, g):
    xf = x.astype(jnp.float32)
    y = xf * lax.rsqrt(jnp.mean(xf * xf, axis=-1, keepdims=True) + EPS)
    return (y * g.astype(jnp.float32)).astype(x.dtype)


def causal_dwconv(x, buf, w, b):
    t = x.shape[1]
    xp = jnp.concatenate([buf.astype(x.dtype), x], axis=1)
    y = b + xp[:, 0:t] * w[0]
    for j in range(1, w.shape[0]):
        y = y + xp[:, j:j + t] * w[j]
    return y, xp[:, t:]


def gla_chunked(q, k, v, log_alpha, s0):
    bsz, t, nh, dk = q.shape
    dv = v.shape[-1]
    c = math.gcd(t, GLA_CHUNK)
    nc = t // c

    def to_chunks(z):
        return z.reshape(bsz, nc, c, nh, z.shape[-1]).transpose(1, 0, 3, 2, 4).astype(jnp.float32)

    mask = jnp.tril(jnp.ones((c, c), dtype=bool))

    def step(s, inp):
        qi, ki, vi, gi = inp
        b = jnp.cumsum(gi, axis=2)
        qd = qi * jnp.exp(b)
        kd = ki * jnp.exp(-b)
        att = jnp.where(mask, jnp.einsum('bhtd,bhsd->bhts', qd, kd), 0.0)
        o = jnp.einsum('bhts,bhsv->bhtv', att, vi) + jnp.einsum('bhtd,bhdv->bhtv', qd, s)
        b_last = b[:, :, -1:, :]
        s = jnp.exp(b_last[:, :, 0, :])[..., None] * s + jnp.einsum('bhsd,bhsv->bhdv', ki * jnp.exp(b_last - b), vi)
        return s, o

    s, o = lax.scan(step, s0.astype(jnp.float32), (to_chunks(q), to_chunks(k), to_chunks(v), to_chunks(log_alpha)))
    o = o.transpose(1, 0, 3, 2, 4).reshape(bsz, t, nh, dv)
    return o.astype(v.dtype), s.astype(s0.dtype)


def linear_scan(a, bx, h0):
    def combine(l, r):
        return l[0] * r[0], r[0] * l[1] + r[1]
    a_cum, b_cum = lax.associative_scan(combine, (a, bx), axis=1)
    return b_cum + a_cum * h0[:, None, :]


def mixer_a(h, gla_s0, lru_buf, lru_h0, w_in, w_gk2, b_gk2, gla_norm_g, conv_w, conv_b, gate_w, gate_b, lam, w_out):
    bsz, t, _ = h.shape
    q, k, v, gk_lo, g, xb, yb = jnp.split(h @ w_in, A_SPLITS, axis=-1)
    q = q.reshape(bsz, t, GLA_HEADS, GLA_DK) * (GLA_DK ** -0.5)
    k = k.reshape(bsz, t, GLA_HEADS, GLA_DK)
    v = v.reshape(bsz, t, GLA_HEADS, GLA_DV)
    log_alpha = (jax.nn.log_sigmoid((gk_lo @ w_gk2 + b_gk2).astype(jnp.float32)) / GLA_GATE_NORM).reshape(bsz, t, GLA_HEADS, GLA_DK)
    o, gla_s = gla_chunked(q, k, v, log_alpha, gla_s0)
    o = (rmsnorm(o, gla_norm_g) * jax.nn.silu(g.reshape(bsz, t, GLA_HEADS, GLA_DV))).reshape(bsz, t, A_V)
    xc, lru_buf_new = causal_dwconv(xb, lru_buf, conv_w, conv_b)
    gates = jnp.einsum('btnc,gncd->gbtnd', xc.reshape(bsz, t, LRU_BLOCKS, LRU_BLOCK_W), gate_w).reshape(2, bsz, t, LRU_WIDTH) + gate_b[:, None, None, :]
    gates = jax.nn.sigmoid(gates.astype(jnp.float32))
    log_a = -LRU_C * gates[0] * jax.nn.softplus(-lam.astype(jnp.float32))
    bx = jnp.sqrt(-jnp.expm1(2.0 * log_a)) * gates[1] * xc.astype(jnp.float32)
    hs = linear_scan(jnp.exp(log_a), bx, lru_h0.astype(jnp.float32))
    lru_out = hs.astype(h.dtype) * jax.nn.gelu(yb, approximate=True)
    out = jnp.concatenate([o, lru_out], axis=-1) @ w_out
    return out, gla_s, lru_buf_new, hs[:, -1].astype(lru_h0.dtype)


def fox_prompt(q, k, v, logf):
    bsz, t, nh, hd = q.shape
    nb = t // FOX_BLOCK
    c_t = jnp.cumsum(logf, axis=1).transpose(0, 2, 1)
    qb = q.reshape(bsz, nb, FOX_BLOCK, nh, hd).transpose(1, 0, 2, 3, 4)
    cb = c_t.reshape(bsz, nh, nb, FOX_BLOCK).transpose(2, 0, 1, 3)
    kpos = jnp.arange(t)

    def block(inp):
        qi, ci, i = inp
        s = jnp.einsum('bqhd,bkhd->bhqk', qi, k).astype(jnp.float32) * FOX_SCALE
        s = s + ci[..., None] - c_t[:, :, None, :]
        qpos = i * FOX_BLOCK + jnp.arange(FOX_BLOCK)
        s = jnp.where(kpos[None, :] <= qpos[:, None], s, -jnp.inf)
        pr = jax.nn.softmax(s, axis=-1)
        return jnp.einsum('bhqk,bkhd->bqhd', pr.astype(v.dtype), v)

    o = lax.map(block, (qb, cb, jnp.arange(nb)))
    return o.transpose(1, 0, 2, 3, 4).reshape(bsz, t, nh, hd)


def fox_sample(q, k, v, logf, cache_k, cache_v, cache_logf, page_table, layer):
    bsz, t, nh, hd = q.shape
    n_pages = page_table.shape[1]
    past = cache_logf[page_table, layer].astype(jnp.float32).reshape(bsz, n_pages * PAGE_SIZE, nh)
    suffix = lax.cumsum(past, axis=1, reverse=True) - past
    bias_pages = suffix.reshape(bsz, n_pages, PAGE_SIZE, nh).transpose(1, 0, 3, 2)
    cn = jnp.cumsum(logf, axis=1).transpose(0, 2, 1)
    s = jnp.einsum('bqhd,bkhd->bhqk', q, k).astype(jnp.float32) * FOX_SCALE + cn[..., :, None] - cn[..., None, :]
    s = jnp.where(jnp.tril(jnp.ones((t, t), dtype=bool)), s, -jnp.inf)
    m = jnp.max(s, axis=-1)
    pe = jnp.exp(s - m[..., None])
    l = jnp.sum(pe, axis=-1)
    acc = jnp.einsum('bhqk,bkhd->bhqd', pe, v.astype(jnp.float32))

    def page_step(carry, inp):
        m, l, acc = carry
        pidx, bias = inp
        kp = cache_k[pidx, layer]
        vp = cache_v[pidx, layer]
        s = jnp.einsum('bqhd,bkhd->bhqk', q, kp).astype(jnp.float32) * FOX_SCALE + cn[..., :, None] + bias[:, :, None, :]
        m_new = jnp.maximum(m, jnp.max(s, axis=-1))
        corr = jnp.exp(m - m_new)
        pe = jnp.exp(s - m_new[..., None])
        l = l * corr + jnp.sum(pe, axis=-1)
        acc = acc * corr[..., None] + jnp.einsum('bhqk,bkhd->bhqd', pe, vp.astype(jnp.float32))
        return (m_new, l, acc), None

    (m, l, acc), _ = lax.scan(page_step, (m, l, acc), (page_table.T, bias_pages))
    o = acc / l[..., None]
    return o.transpose(0, 2, 1, 3).astype(v.dtype)


def mixer_c(h, attend, w_in, b_f, w_out):
    bsz, t, _ = h.shape
    q, k, v, f_logit = jnp.split(h @ w_in, C_SPLITS, axis=-1)
    shp = (bsz, t, FOX_HEADS, FOX_HD)
    q, k, v = q.reshape(shp), k.reshape(shp), v.reshape(shp)
    logf = jax.nn.log_sigmoid((f_logit + b_f).astype(jnp.float32))
    o = attend(q, k, v, logf)
    return o.reshape(bsz, t, D_ATT) @ w_out, k, v, logf.astype(h.dtype)


def conv_ffn(h, buf, w_up, conv_w, conv_b, w_down):
    g, u = jnp.split(h @ w_up, 2, axis=-1)
    gc, buf_new = causal_dwconv(g, buf, conv_w, conv_b)
    return (jax.nn.gelu(gc, approximate=True) * u) @ w_down, buf_new


def trunk(x, gla_s, lru_buf, lru_h, ffn_buf, attend, prm):
    gla_l, buf_l, h_l, k_l, v_l, lf_l, ffn_l = [], [], [], [], [], [], []
    ia = 0
    ic = 0
    for layer in range(DEPTH):
        ng = prm['norm_g'][layer]
        h = rmsnorm(x, ng[0])
        if layer % 2 == 0:
            out, s_new, b_new, h_new = mixer_a(h, gla_s[:, ia], lru_buf[:, ia], lru_h[:, ia], prm['w_in_a'][ia], prm['w_gk2'][ia], prm['b_gk2'][ia], prm['gla_norm_g'][ia], prm['lru_conv_w'][ia], prm['lru_conv_b'][ia], prm['lru_gate_w'][ia], prm['lru_gate_b'][ia], prm['lru_lambda'][ia], prm['w_out_a'][ia])
            gla_l.append(s_new)
            buf_l.append(b_new)
            h_l.append(h_new)
            ia += 1
        else:
            out, k_new, v_new, lf_new = mixer_c(h, functools.partial(attend, layer=ic), prm['w_in_c'][ic], prm['b_f'][ic], prm['w_out_c'][ic])
            k_l.append(k_new)
            v_l.append(v_new)
            lf_l.append(lf_new)
            ic += 1
        x = x + rmsnorm(out, ng[1])
        f, fb = conv_ffn(rmsnorm(x, ng[2]), ffn_buf[:, layer], prm['ffn_w_up'][layer], prm['ffn_conv_w'][layer], prm['ffn_conv_b'][layer], prm['ffn_w_down'][layer])
        ffn_l.append(fb)
        x = x + rmsnorm(f, ng[3])
    return (x, jnp.stack(gla_l, axis=1), jnp.stack(buf_l, axis=1), jnp.stack(h_l, axis=1), jnp.stack(k_l, axis=1), jnp.stack(v_l, axis=1), jnp.stack(lf_l, axis=1), jnp.stack(ffn_l, axis=1))


def setup_inputs(seed: int = 0) -> dict:
    key = jax.random.key(seed)
    ks = jax.random.split(key, 32)
    f32 = jnp.float32

    def nrm(k, shape, scale):
        return jax.random.normal(k, shape, f32) * scale

    n_pages = PAST_LEN // PAGE_SIZE
    n_used = DEC_BATCH * n_pages
    n_pool = n_used + max(1, n_used // 4)
    u = jax.random.uniform(ks[20], (N_A_LAYERS, LRU_WIDTH), f32, 0.9, 0.999)
    a0 = u ** (1.0 / LRU_C)
    return {
        'x_prompt': nrm(ks[0], (BATCH, SEQ, D_MODEL), 1.0),
        'x_sample': nrm(ks[1], (DEC_BATCH, DEC_SEQ, D_MODEL), 1.0),
        'state_gla': nrm(ks[2], (DEC_BATCH, N_A_LAYERS, GLA_HEADS, GLA_DK, GLA_DV), 0.5),
        'state_lru_conv': nrm(ks[3], (DEC_BATCH, N_A_LAYERS, LRU_CONV_W - 1, LRU_WIDTH), 1.0),
        'state_lru_h': nrm(ks[4], (DEC_BATCH, N_A_LAYERS, LRU_WIDTH), 0.5),
        'cache_k': nrm(ks[5], (n_pool, N_C_LAYERS, PAGE_SIZE, FOX_HEADS, FOX_HD), 1.0),
        'cache_v': nrm(ks[6], (n_pool, N_C_LAYERS, PAGE_SIZE, FOX_HEADS, FOX_HD), 1.0),
        'cache_logf': jax.nn.log_sigmoid(nrm(ks[7], (n_pool, N_C_LAYERS, PAGE_SIZE, FOX_HEADS), 0.5) + CACHE_LOGIT_CENTRE),
        'state_ffn_conv': nrm(ks[8], (DEC_BATCH, DEPTH, FFN_CONV_W - 1, D_FF), 1.0),
        'page_table': jax.random.permutation(ks[9], n_pool)[:n_used].reshape(DEC_BATCH, n_pages).astype(jnp.int32),
        'norm_g': 1.0 + nrm(ks[10], (DEPTH, 4, D_MODEL), 0.05),
        'w_in_a': nrm(ks[11], (N_A_LAYERS, D_MODEL, IN_A), D_MODEL ** -0.5),
        'w_gk2': nrm(ks[12], (N_A_LAYERS, GLA_GK_RANK, A_Q), GLA_GK_RANK ** -0.5),
        'b_gk2': nrm(ks[13], (N_A_LAYERS, A_Q), 0.1),
        'gla_norm_g': 1.0 + nrm(ks[14], (N_A_LAYERS, GLA_DV), 0.05),
        'lru_conv_w': nrm(ks[15], (N_A_LAYERS, LRU_CONV_W, LRU_WIDTH), LRU_CONV_W ** -0.5),
        'lru_conv_b': nrm(ks[16], (N_A_LAYERS, LRU_WIDTH), 0.02),
        'lru_gate_w': nrm(ks[17], (N_A_LAYERS, 2, LRU_BLOCKS, LRU_BLOCK_W, LRU_BLOCK_W), LRU_BLOCK_W ** -0.5),
        'lru_gate_b': nrm(ks[18], (N_A_LAYERS, 2, LRU_WIDTH), 0.02),
        'lru_lambda': jnp.log(a0) - jnp.log1p(-a0),
        'w_out_a': nrm(ks[19], (N_A_LAYERS, MIX_A, D_MODEL), MIX_A ** -0.5),
        'w_in_c': nrm(ks[21], (N_C_LAYERS, D_MODEL, IN_C), D_MODEL ** -0.5),
        'b_f': FOX_BIAS_INIT + nrm(ks[22], (N_C_LAYERS, FOX_HEADS), 0.5),
        'w_out_c': nrm(ks[23], (N_C_LAYERS, D_ATT, D_MODEL), D_ATT ** -0.5),
        'ffn_w_up': nrm(ks[24], (DEPTH, D_MODEL, 2 * D_FF), D_MODEL ** -0.5),
        'ffn_conv_w': nrm(ks[25], (DEPTH, FFN_CONV_W, D_FF), FFN_CONV_W ** -0.5),
        'ffn_conv_b': nrm(ks[26], (DEPTH, D_FF), 0.02),
        'ffn_w_down': nrm(ks[27], (DEPTH, D_FF, D_MODEL), D_FF ** -0.5),
    }


def reference(x_prompt, x_sample, state_gla, state_lru_conv, state_lru_h, cache_k, cache_v, cache_logf, state_ffn_conv, page_table, norm_g, w_in_a, w_gk2, b_gk2, gla_norm_g, lru_conv_w, lru_conv_b, lru_gate_w, lru_gate_b, lru_lambda, w_out_a, w_in_c, b_f, w_out_c, ffn_w_up, ffn_conv_w, ffn_conv_b, ffn_w_down):
    prm = {'norm_g': norm_g, 'w_in_a': w_in_a, 'w_gk2': w_gk2, 'b_gk2': b_gk2, 'gla_norm_g': gla_norm_g, 'lru_conv_w': lru_conv_w, 'lru_conv_b': lru_conv_b, 'lru_gate_w': lru_gate_w, 'lru_gate_b': lru_gate_b, 'lru_lambda': lru_lambda, 'w_out_a': w_out_a, 'w_in_c': w_in_c, 'b_f': b_f, 'w_out_c': w_out_c, 'ffn_w_up': ffn_w_up, 'ffn_conv_w': ffn_conv_w, 'ffn_conv_b': ffn_conv_b, 'ffn_w_down': ffn_w_down}
    bp = x_prompt.shape[0]
    dt = x_prompt.dtype
    gla0 = jnp.zeros((bp, N_A_LAYERS, GLA_HEADS, GLA_DK, GLA_DV), dt)
    lru_buf0 = jnp.zeros((bp, N_A_LAYERS, LRU_CONV_W - 1, LRU_WIDTH), dt)
    lru_h0 = jnp.zeros((bp, N_A_LAYERS, LRU_WIDTH), dt)
    ffn_buf0 = jnp.zeros((bp, DEPTH, FFN_CONV_W - 1, D_FF), dt)

    def attend_prompt(q, k, v, logf, layer):
        return fox_prompt(q, k, v, logf)

    attend_sample = functools.partial(fox_sample, cache_k=cache_k, cache_v=cache_v, cache_logf=cache_logf, page_table=page_table)

    y_prompt, gla_p, lru_conv_p, lru_h_p, k_p, v_p, logf_p, ffn_conv_p = trunk(x_prompt, gla0, lru_buf0, lru_h0, ffn_buf0, attend_prompt, prm)
    y_sample, gla_s, lru_conv_s, lru_h_s, k_s, v_s, logf_s, ffn_conv_s = trunk(x_sample, state_gla, state_lru_conv, state_lru_h, state_ffn_conv, attend_sample, prm)
    return (y_prompt, y_sample, gla_p, gla_s, lru_conv_p, lru_conv_s, lru_h_p, lru_h_s, k_p, k_s, v_p, v_s, logf_p, logf_s, ffn_conv_p, ffn_conv_s)
```

```python
import functools

import numpy as np
import jax
import jax.numpy as jnp
from jax import lax
from jax.experimental import pallas as pl
from jax.experimental.pallas import tpu as pltpu

F32 = jnp.float32
BF16 = jnp.bfloat16
HIGHEST = lax.Precision.HIGHEST

EPS = 1e-6
GLA_HEADS = 4
GLA_DK = 64
GLA_DV = 128
GLA_GATE_NORM = 16.0
GLA_CHUNK = 64
LRU_C = 8.0
FOX_HEADS = 16
FOX_HD = 64
FOX_SCALE = FOX_HD ** -0.5
PAGE_SIZE = 128
LANES = 128
SUBLANES = 8
HIST = SUBLANES
VMEM_LIMIT = 56 * 2 ** 20
PAGES_PER_STEP = 4
NEG_INF = float("-inf")


def _rms(x, g):
    return x * lax.rsqrt(jnp.mean(x * x, axis=-1, keepdims=True) + EPS) * g


def _sigmoid(x):
    return 1.0 / (1.0 + jnp.exp(-x))


def _softplus(x):
    return jnp.maximum(x, 0.0) + jnp.log1p(jnp.exp(-jnp.abs(x)))


def _log_sigmoid(x):
    return -_softplus(-x)


def _expm1(x):
    u = jnp.exp(x)
    corrected = (jnp.abs(x) < 0.5) & (u != 1.0)
    ratio = x / jnp.log(jnp.where(corrected, u, 2.0))
    return jnp.where(u == 1.0, x, (u - 1.0) * jnp.where(corrected, ratio, 1.0))


def _gelu(x):
    return x * (0.5 * (1.0 + jnp.tanh(0.7978845608028654 * (x + 0.044715 * (x * x * x)))))


def _dot(a, b, precision=None):
    return jnp.dot(a, b, preferred_element_type=F32, precision=precision)


def _dot_nt(a, b, precision=None):
    return lax.dot_general(a, b, (((1,), (1,)), ((), ())), preferred_element_type=F32, precision=precision)


def _dot_tn(a, b, precision=None):
    return lax.dot_general(a, b, (((0,), (0,)), ((), ())), preferred_element_type=F32, precision=precision)


def _seq_tril(m, l):
    row = lax.broadcasted_iota(jnp.int32, (m, m), 0)
    col = lax.broadcasted_iota(jnp.int32, (m, m), 1)
    same = (row & -l) == (col & -l)
    return jnp.where(same & (col <= row), 1.0, 0.0).astype(F32)


def _params(sem):
    return pltpu.CompilerParams(dimension_semantics=sem, vmem_limit_bytes=VMEM_LIMIT)


def _tile_spec(s, l, n):
    return pl.BlockSpec((s, l, n), lambda b, t: (b, t, 0))


def _seq_spec(s, r, n):
    return pl.BlockSpec((s, r, n), lambda b, t: (b, 0, 0))


def _const_spec(shape):
    return pl.BlockSpec(shape, lambda b, t: (0,) * len(shape))


def _norm_proj_body(x_ref, g_ref, w_ref, *o_refs):
    s, l, d = x_ref.shape
    h = _rms(x_ref[...].reshape(s * l, d), g_ref[...]).astype(BF16)
    off = 0
    for o_ref in o_refs:
        n = o_ref.shape[-1]
        o_ref[...] = _dot(h, w_ref[:, off:off + n]).reshape(s, l, n).astype(o_ref.dtype)
        off += n


def _norm_proj(x, g, w, widths, s, l):
    b, t, d = x.shape
    return pl.pallas_call(
        _norm_proj_body,
        grid=(b // s, t // l),
        in_specs=[_tile_spec(s, l, d), _const_spec((1, d)), _const_spec(w.shape)],
        out_specs=[_tile_spec(s, l, n) for n in widths],
        out_shape=[jax.ShapeDtypeStruct((b, t, n), F32) for n in widths],
        compiler_params=_params(("parallel", "parallel")),
        name="norm_proj_a",
    )(x, g, w)


def _proj_c_body(x_ref, g_ref, w_ref, bf_ref, q_ref, k_ref, v_ref, kb_ref, vb_ref, lf_ref, c_ref, carry_ref):
    s, l, d = x_ref.shape
    m = s * l
    da = q_ref.shape[-1]
    nh = lf_ref.shape[-1]
    h = _rms(x_ref[...].reshape(m, d), g_ref[...]).astype(BF16)
    q_ref[...] = _dot(h, w_ref[:, 0:da]).reshape(s, l, da).astype(q_ref.dtype)
    k = _dot(h, w_ref[:, da:2 * da]).reshape(s, l, da)
    k_ref[...] = k
    kb_ref[...] = k.astype(kb_ref.dtype)
    v = _dot(h, w_ref[:, 2 * da:3 * da]).reshape(s, l, da)
    v_ref[...] = v
    vb_ref[...] = v.astype(vb_ref.dtype)
    lf = _log_sigmoid(_dot(h, w_ref[:, 3 * da:]) + bf_ref[...])

    @pl.when(pl.program_id(1) == 0)
    def _():
        carry_ref[...] = jnp.zeros_like(carry_ref)

    c = _dot(_seq_tril(m, l), lf, HIGHEST) + carry_ref[...]
    carry_ref[...] = c[m - 1:m, :]
    lf_ref[...] = lf[:, :nh].reshape(s, l, nh)
    c_ref[...] = c[:, :nh].reshape(s, l, nh)


def _proj_c(x, g, w, bf, s, l):
    b, t, d = x.shape
    assert s == 1 or t == l
    da = FOX_HEADS * FOX_HD
    f32_out = jax.ShapeDtypeStruct((b, t, da), F32)
    bf_out = jax.ShapeDtypeStruct((b, t, da), BF16 if l % (2 * SUBLANES) == 0 else F32)
    h_out = jax.ShapeDtypeStruct((b, t, FOX_HEADS), F32)
    return pl.pallas_call(
        _proj_c_body,
        grid=(b // s, t // l),
        in_specs=[_tile_spec(s, l, d), _const_spec((1, d)), _const_spec(w.shape), _const_spec((1, LANES))],
        out_specs=[_tile_spec(s, l, da)] * 5 + [_tile_spec(s, l, FOX_HEADS)] * 2,
        out_shape=[bf_out, f32_out, f32_out, bf_out, bf_out, h_out, h_out],
        scratch_shapes=[pltpu.VMEM((1, LANES), F32)],
        compiler_params=_params(("parallel", "arbitrary")),
        name="norm_proj_c",
    )(x, g, w, bf)


def _gla_body(q_ref, k_ref, v_ref, lo_ref, gg_ref, wgk_ref, bgk_ref, ng_ref, s0_ref, o_ref, sn_ref, st_ref, *, chunk):
    s, l, dq = q_ref.shape
    nc = l // chunk
    mm = BF16 if chunk >= 16 else F32
    t = pl.program_id(1)

    @pl.when(t == 0)
    def _():
        st_ref[...] = s0_ref[...]

    tri = _seq_tril(chunk, chunk)
    lane = lax.broadcasted_iota(jnp.int32, (1, dq), 1)
    wgk = wgk_ref[...].astype(mm)
    bgk = bgk_ref[...]
    ng = ng_ref[...]

    def do_chunk(si, rows):
        qi = q_ref[si, rows, :] * (GLA_DK ** -0.5)
        ki = k_ref[si, rows, :]
        vi = v_ref[si, rows, :]
        gi = gg_ref[si, rows, :]
        lo = lo_ref[si, rows, :]
        la = _log_sigmoid(_dot(lo.astype(mm), wgk) + bgk) / GLA_GATE_NORM
        bcum = _dot(tri, la, HIGHEST)
        blast = bcum[chunk - 1:chunk, :]
        qd = qi * jnp.exp(bcum)
        kd = (ki * jnp.exp(-bcum)).astype(mm)
        kd2 = (ki * jnp.exp(blast - bcum)).astype(mm)
        st = st_ref[si]
        stm = st.astype(mm)
        upd = jnp.zeros_like(st)
        for h in range(GLA_HEADS):
            hm = (lane >= h * GLA_DK) & (lane < (h + 1) * GLA_DK)
            qm = jnp.where(hm, qd, 0.0).astype(mm)
            att = jnp.where(tri > 0.0, _dot_nt(qm, kd), 0.0)
            vh = vi[:, h * GLA_DV:(h + 1) * GLA_DV].astype(mm)
            o = _dot(att.astype(mm), vh) + _dot_nt(qm, stm)
            upd = jnp.where(hm, _dot_tn(vh, kd2), upd)
            gh = gi[:, h * GLA_DV:(h + 1) * GLA_DV]
            o_ref[si, rows, h * GLA_DV:(h + 1) * GLA_DV] = (_rms(o, ng) * (gh * _sigmoid(gh))).astype(o_ref.dtype)
        st_ref[si] = st * jnp.exp(blast) + upd

    def do_seq(si, carry):
        if nc == 1:
            do_chunk(si, slice(0, chunk))
        else:
            def body(ci, c2):
                do_chunk(si, pl.ds(pl.multiple_of(ci * chunk, chunk), chunk))
                return c2
            lax.fori_loop(0, nc, body, 0)
        return carry

    if s == 1:
        do_seq(0, 0)
    else:
        lax.fori_loop(0, s, do_seq, 0)

    @pl.when(t == pl.num_programs(1) - 1)
    def _():
        sn_ref[...] = st_ref[...]


def _gla(q, k, v, lo, gg, wgk, bgk, ng, s0t, s, l, chunk):
    b, t, dq = q.shape
    dv = v.shape[-1]
    return pl.pallas_call(
        functools.partial(_gla_body, chunk=chunk),
        grid=(b // s, t // l),
        in_specs=[_tile_spec(s, l, dq), _tile_spec(s, l, dq), _tile_spec(s, l, dv), _tile_spec(s, l, LANES),
                  _tile_spec(s, l, dv), _const_spec(wgk.shape), _const_spec((1, dq)), _const_spec((1, GLA_DV)),
                  _seq_spec(s, GLA_DV, dq)],
        out_specs=[_tile_spec(s, l, dv), _seq_spec(s, GLA_DV, dq)],
        out_shape=[jax.ShapeDtypeStruct((b, t, dv), F32), jax.ShapeDtypeStruct((b, GLA_DV, dq), F32)],
        scratch_shapes=[pltpu.VMEM((s, GLA_DV, dq), F32)],
        compiler_params=_params(("parallel", "arbitrary")),
        name="gla",
    )(q, k, v, lo, gg, wgk, bgk, ng, s0t)


def _lru_body(xb_ref, yb_ref, cbuf_ref, h0_ref, cw_ref, cb_ref, gw_ref, gb_ref, lam_ref,
              o_ref, ncbuf_ref, nh_ref, xbuf, hc):
    s, l, c = xb_ref.shape
    m = s * l
    nw = cw_ref.shape[0]
    t = pl.program_id(1)

    @pl.when(t == 0)
    def _():
        xbuf[:, HIST - (nw - 1):HIST, :] = cbuf_ref[...]
        hc[...] = h0_ref[...]

    @pl.when(t > 0)
    def _():
        xbuf[:, 0:HIST, :] = xbuf[:, l:l + HIST, :]

    xbuf[:, HIST:HIST + l, :] = xb_ref[...]
    xc = cb_ref[...].reshape(1, 1, c)
    for j in range(nw):
        off = HIST - (nw - 1) + j
        xc = xc + xbuf[:, off:off + l, :] * cw_ref[j:j + 1, :].reshape(1, 1, c)
    xc = xc.reshape(m, c)
    gates = _dot(xc.astype(BF16), gw_ref[...]) + gb_ref[...]
    r = _sigmoid(gates[:, :c])
    i = _sigmoid(gates[:, c:])
    log_a = (-LRU_C * r) * _softplus(-lam_ref[...])
    a = jnp.exp(log_a)
    bx = jnp.sqrt(-_expm1(2.0 * log_a)) * i * xc
    pos = lax.broadcasted_iota(jnp.int32, (m, c), 0) & (l - 1)
    d = 1
    while d < l:
        valid = pos >= d
        a_prev = jnp.where(valid, pltpu.roll(a, d, 0), 1.0)
        b_prev = jnp.where(valid, pltpu.roll(bx, d, 0), 0.0)
        bx = a * b_prev + bx
        a = a * a_prev
        d *= 2
    hin = jnp.broadcast_to(hc[...], (s, l, c)).reshape(m, c)
    hs = bx + a * hin
    o_ref[...] = (hs * _gelu(yb_ref[...].reshape(m, c))).reshape(s, l, c)
    hc[...] = hs.reshape(s, l, c)[:, l - 1:l, :]

    @pl.when(t == pl.num_programs(1) - 1)
    def _():
        ncbuf_ref[...] = xbuf[:, l + HIST - (nw - 1):l + HIST, :]
        nh_ref[...] = hc[...]


def _lru(xb, yb, cbuf, h0, cw, cb, gw, gb, lam, s, l):
    b, t, c = xb.shape
    nw = cw.shape[0]
    return pl.pallas_call(
        _lru_body,
        grid=(b // s, t // l),
        in_specs=[_tile_spec(s, l, c), _tile_spec(s, l, c), _seq_spec(s, nw - 1, c), _seq_spec(s, 1, c),
                  _const_spec(cw.shape), _const_spec((1, c)), _const_spec(gw.shape), _const_spec((1, 2 * c)),
                  _const_spec((1, c))],
        out_specs=[_tile_spec(s, l, c), _seq_spec(s, nw - 1, c), _seq_spec(s, 1, c)],
        out_shape=[jax.ShapeDtypeStruct((b, t, c), F32), jax.ShapeDtypeStruct((b, nw - 1, c), F32),
                   jax.ShapeDtypeStruct((b, 1, c), F32)],
        scratch_shapes=[pltpu.VMEM((s, l + HIST, c), F32), pltpu.VMEM((s, 1, c), F32)],
        compiler_params=_params(("parallel", "arbitrary")),
        name="rg_lru",
    )(xb, yb, cbuf, h0, cw, cb, gw, gb, lam)


def _out_proj_body(*refs, n_in):
    a_refs, w_refs = refs[:n_in], refs[n_in:2 * n_in]
    x_ref, g_ref, y_ref = refs[2 * n_in:]
    s, l, d = x_ref.shape
    m = s * l
    acc = None
    for a_ref, w_ref in zip(a_refs, w_refs):
        part = _dot(a_ref[...].reshape(m, a_ref.shape[-1]).astype(BF16), w_ref[...])
        acc = part if acc is None else acc + part
    y_ref[...] = (x_ref[...].reshape(m, d) + _rms(acc, g_ref[...])).reshape(s, l, d)


def _out_proj(acts, ws, x, g, s, l):
    b, t, d = x.shape
    n_in = len(acts)
    return pl.pallas_call(
        functools.partial(_out_proj_body, n_in=n_in),
        grid=(b // s, t // l),
        in_specs=[_tile_spec(s, l, a.shape[-1]) for a in acts] + [_const_spec(w.shape) for w in ws]
                 + [_tile_spec(s, l, d), _const_spec((1, d))],
        out_specs=_tile_spec(s, l, d),
        out_shape=jax.ShapeDtypeStruct((b, t, d), F32),
        compiler_params=_params(("parallel", "parallel")),
        name="out_proj",
    )(*acts, *ws, x, g)


def _ffn_body(x_ref, buf_ref, g2_ref, g3_ref, wup_ref, cw_ref, cb_ref, wdn_ref, y_ref, nbuf_ref, gbuf):
    s, l, d = x_ref.shape
    m = s * l
    nw, f = cw_ref.shape
    t = pl.program_id(1)
    x = x_ref[...].reshape(m, d)
    h = _rms(x, g2_ref[...]).astype(BF16)

    @pl.when(t == 0)
    def _():
        gbuf[:, HIST - (nw - 1):HIST, :] = buf_ref[...]

    @pl.when(t > 0)
    def _():
        gbuf[:, 0:HIST, :] = gbuf[:, l:l + HIST, :]

    gbuf[:, HIST:HIST + l, :] = _dot(h, wup_ref[:, :f]).reshape(s, l, f)
    u = _dot(h, wup_ref[:, f:])
    gc = cb_ref[...].reshape(1, 1, f)
    for j in range(nw):
        off = HIST - (nw - 1) + j
        gc = gc + gbuf[:, off:off + l, :] * cw_ref[j:j + 1, :].reshape(1, 1, f)
    act = (_gelu(gc).reshape(m, f) * u).astype(BF16)
    y = _dot(act, wdn_ref[...])
    y_ref[...] = (x + _rms(y, g3_ref[...])).reshape(s, l, d)

    @pl.when(t == pl.num_programs(1) - 1)
    def _():
        nbuf_ref[...] = gbuf[:, l + HIST - (nw - 1):l + HIST, :]


def _ffn(x, buf, g2, g3, wup, cw, cb, wdn, s, l):
    b, t, d = x.shape
    nw, f = cw.shape
    return pl.pallas_call(
        _ffn_body,
        grid=(b // s, t // l),
        in_specs=[_tile_spec(s, l, d), _seq_spec(s, nw - 1, f), _const_spec((1, d)), _const_spec((1, d)),
                  pl.BlockSpec(wup.shape, lambda b_, t_: (0, 0), pipeline_mode=pl.Buffered(1)),
                  _const_spec(cw.shape), _const_spec((1, f)),
                  pl.BlockSpec(wdn.shape, lambda b_, t_: (0, 0), pipeline_mode=pl.Buffered(1))],
        out_specs=[_tile_spec(s, l, d), _seq_spec(s, nw - 1, f)],
        out_shape=[jax.ShapeDtypeStruct((b, t, d), F32), jax.ShapeDtypeStruct((b, nw - 1, f), F32)],
        scratch_shapes=[pltpu.VMEM((s, l + HIST, f), F32)],
        compiler_params=_params(("parallel", "arbitrary")),
        name="conv_ffn",
    )(x, buf, g2, g3, wup, cw, cb, wdn)


def _fox_prompt_body(q_ref, k_ref, v_ref, c_ref, ct_ref, o_ref, m_ref, l_ref, acc_ref, *, tq):
    hp = pl.program_id(1)
    iq = pl.program_id(2)
    lane = lax.broadcasted_iota(jnp.int32, (1, LANES), 1)
    lo_half = lane < FOX_HD
    q = q_ref[0] * FOX_SCALE
    c_blk = c_ref[0]
    lane_h = lax.broadcasted_iota(jnp.int32, c_blk.shape, 1)
    qs, cts = [], []
    for j in range(2):
        qs.append(jnp.where(lo_half if j == 0 else ~lo_half, q, jnp.zeros_like(q)))
        cts.append(jnp.sum(jnp.where(lane_h == 2 * hp + j, c_blk, 0.0), axis=-1, keepdims=True))
    m_ref[...] = jnp.full_like(m_ref, NEG_INF)
    l_ref[...] = jnp.zeros_like(l_ref)
    acc_ref[...] = jnp.zeros_like(acc_ref)
    row = lax.broadcasted_iota(jnp.int32, (tq, tq), 0)
    col = lax.broadcasted_iota(jnp.int32, (tq, tq), 1)

    def step(ik, masked):
        off = pl.multiple_of(ik * tq, tq)
        kt = k_ref[0, pl.ds(off, tq), :]
        vt = v_ref[0, pl.ds(off, tq), :]
        alphas, pvs = [], []
        for j in range(2):
            cs = ct_ref[0, ik, pl.ds(2 * hp + j, 1), :]
            sc = _dot_nt(qs[j], kt) + cts[j] - cs
            if masked:
                sc = jnp.where(col <= row, sc, NEG_INF)
            m_old = m_ref[j]
            m_new = jnp.maximum(m_old, jnp.max(sc, axis=-1, keepdims=True))
            alpha = jnp.exp(m_old - m_new)
            p = jnp.exp(sc - m_new)
            l_ref[j] = alpha * l_ref[j] + jnp.sum(p, axis=-1, keepdims=True)
            m_ref[j] = m_new
            alphas.append(alpha)
            pvs.append(_dot(p.astype(BF16), vt))
        acc_ref[...] = acc_ref[...] * jnp.where(lo_half, alphas[0], alphas[1]) + jnp.where(lo_half, pvs[0], pvs[1])

    def loop_body(ik, carry):
        step(ik, False)
        return carry

    lax.fori_loop(0, iq, loop_body, 0)
    step(iq, True)
    o_ref[0] = (acc_ref[...] / jnp.where(lo_half, l_ref[0], l_ref[1])).astype(o_ref.dtype)


def _fox_prompt(qb, kb, vb, c, tq):
    b, t, da = qb.shape
    nq = t // tq
    ct = c.reshape(b, nq, tq, FOX_HEADS).transpose(0, 1, 3, 2)
    return pl.pallas_call(
        functools.partial(_fox_prompt_body, tq=tq),
        grid=(b, FOX_HEADS // 2, nq),
        in_specs=[pl.BlockSpec((1, tq, LANES), lambda b_, h_, i_: (b_, i_, h_)),
                  pl.BlockSpec((1, t, LANES), lambda b_, h_, i_: (b_, 0, h_)),
                  pl.BlockSpec((1, t, LANES), lambda b_, h_, i_: (b_, 0, h_)),
                  pl.BlockSpec((1, tq, FOX_HEADS), lambda b_, h_, i_: (b_, i_, 0)),
                  pl.BlockSpec((1, nq, FOX_HEADS, tq), lambda b_, h_, i_: (b_, 0, 0, 0))],
        out_specs=pl.BlockSpec((1, tq, LANES), lambda b_, h_, i_: (b_, i_, h_)),
        out_shape=jax.ShapeDtypeStruct((b, t, da), BF16),
        scratch_shapes=[pltpu.VMEM((2, tq, 1), F32), pltpu.VMEM((2, tq, 1), F32), pltpu.VMEM((tq, LANES), F32)],
        compiler_params=_params(("parallel", "parallel", "arbitrary")),
        name="fox_prompt",
    )(qb, kb, vb, c, ct)


def _fox_sample_body(pt_ref, q_ref, kn_ref, vn_ref, cn_ref, cnr_ref, *rest, npp):
    kp = rest[0:npp]
    vp = rest[npp:2 * npp]
    lp = rest[2 * npp:3 * npp]
    o_ref, m_ref, l_ref, acc_ref, carry_ref, lfpad_ref = rest[3 * npp:]
    del pt_ref
    j = pl.program_id(1)
    t, da = q_ref.shape[1], q_ref.shape[2]
    rows = FOX_HEADS * t
    row_c = lax.broadcasted_iota(jnp.int32, (rows, da), 0)
    col_c = lax.broadcasted_iota(jnp.int32, (rows, da), 1)
    own = (col_c & -FOX_HD) == (row_c & -t) * (FOX_HD // t)
    q8 = q_ref[0] * FOX_SCALE
    qbd = jnp.where(own, jnp.concatenate([q8] * FOX_HEADS, axis=0), 0.0).astype(BF16)
    row_l = lax.broadcasted_iota(jnp.int32, (rows, LANES), 0)
    col_l = lax.broadcasted_iota(jnp.int32, (rows, LANES), 1)
    head_sel = jnp.where(col_l * t == (row_l & -t), 1.0, 0.0).astype(F32)
    cn_rows = cnr_ref[0]

    @pl.when(j == 0)
    def _():
        pad = jnp.zeros((PAGE_SIZE - t, da), F32)
        kn = jnp.concatenate([kn_ref[0], pad], axis=0).astype(BF16)
        vn = jnp.concatenate([vn_ref[0], pad], axis=0).astype(BF16)
        sc = _dot_nt(qbd, kn) + cn_rows - _dot_nt(head_sel, cn_ref[0], HIGHEST)
        sc = jnp.where(col_l <= (row_l & (t - 1)), sc, NEG_INF)
        m0 = jnp.max(sc, axis=-1, keepdims=True)
        p = jnp.exp(sc - m0)
        m_ref[...] = m0
        l_ref[...] = jnp.sum(p, axis=-1, keepdims=True)
        acc_ref[...] = _dot(p.astype(BF16), vn)
        carry_ref[...] = jnp.zeros_like(carry_ref)
        lfpad_ref[...] = jnp.zeros_like(lfpad_ref)

    prow = lax.broadcasted_iota(jnp.int32, (PAGE_SIZE, PAGE_SIZE), 0)
    pcol = lax.broadcasted_iota(jnp.int32, (PAGE_SIZE, PAGE_SIZE), 1)
    later = jnp.where(pcol > prow, 1.0, 0.0).astype(F32)
    scs = []
    for i in range(npp):
        lfpad_ref[:, :FOX_HEADS] = lp[i][...]
        lf = lfpad_ref[...]
        suffix = _dot(later, lf, HIGHEST) + carry_ref[...]
        carry_ref[...] = suffix[0:1, :] + lf[0:1, :]
        scs.append(_dot_nt(qbd, kp[i][...].astype(BF16)) + cn_rows + _dot_nt(head_sel, suffix, HIGHEST))
    m_old = m_ref[...]
    m_new = m_old
    for sc in scs:
        m_new = jnp.maximum(m_new, jnp.max(sc, axis=-1, keepdims=True))
    alpha = jnp.exp(m_old - m_new)
    l_new = alpha * l_ref[...]
    acc = acc_ref[...] * alpha
    for i, sc in enumerate(scs):
        p = jnp.exp(sc - m_new)
        l_new = l_new + jnp.sum(p, axis=-1, keepdims=True)
        acc = acc + _dot(p.astype(BF16), vp[i][...].astype(BF16))
    m_ref[...] = m_new
    l_ref[...] = l_new
    acc_ref[...] = acc

    @pl.when(j == pl.num_programs(1) - 1)
    def _():
        full = jnp.where(own, acc / l_new, 0.0)
        out = full[0:t, :]
        for h in range(1, FOX_HEADS):
            out = out + full[h * t:(h + 1) * t, :]
        o_ref[0] = out


def _fox_sample(q, k, v, c, cache_k, cache_v, cache_logf, page_table, layer):
    b, t, da = q.shape
    n_pages = page_table.shape[1]
    npp = PAGES_PER_STEP
    assert n_pages % npp == 0 and PAGE_SIZE % t == 0 and t % SUBLANES == 0
    n_pool, n_layers = cache_k.shape[0], cache_k.shape[1]
    ck = cache_k.reshape(n_pool, n_layers, PAGE_SIZE, da)
    cv = cache_v.reshape(n_pool, n_layers, PAGE_SIZE, da)
    cn_pad = jnp.pad(c, ((0, 0), (0, PAGE_SIZE - t), (0, LANES - FOX_HEADS)))
    cn_rows = c.transpose(0, 2, 1).reshape(b, FOX_HEADS * t, 1)

    def page_spec(i, width):
        return pl.BlockSpec((None, None, PAGE_SIZE, width),
                            lambda b_, j_, pt: (pt[b_, n_pages - 1 - (j_ * npp + i)], layer, 0, 0))

    tok = pl.BlockSpec((1, t, da), lambda b_, j_, pt: (b_, 0, 0))
    rows = FOX_HEADS * t
    grid_spec = pltpu.PrefetchScalarGridSpec(
        num_scalar_prefetch=1,
        grid=(b, n_pages // npp),
        in_specs=[tok, tok, tok,
                  pl.BlockSpec((1, PAGE_SIZE, LANES), lambda b_, j_, pt: (b_, 0, 0)),
                  pl.BlockSpec((1, rows, 1), lambda b_, j_, pt: (b_, 0, 0))]
                 + [page_spec(i, da) for i in range(npp)] * 2
                 + [page_spec(i, FOX_HEADS) for i in range(npp)],
        out_specs=tok,
        scratch_shapes=[pltpu.VMEM((rows, 1), F32), pltpu.VMEM((rows, 1), F32), pltpu.VMEM((rows, da), F32),
                        pltpu.VMEM((1, LANES), F32), pltpu.VMEM((PAGE_SIZE, LANES), F32)],
    )
    return pl.pallas_call(
        functools.partial(_fox_sample_body, npp=npp),
        grid_spec=grid_spec,
        out_shape=jax.ShapeDtypeStruct((b, t, da), F32),
        compiler_params=_params(("parallel", "arbitrary")),
        name="fox_sample",
    )(page_table, q, k, v, cn_pad, cn_rows, *([ck] * npp), *([cv] * npp), *([cache_logf] * npp))


def _pow2_tile(t, want):
    l = min(t, want)
    assert t % l == 0 and l & (l - 1) == 0
    return l


def _prep_weights(p):
    w = {}
    n_a = p['w_in_a'].shape[0]
    wa = p['w_in_a']
    d = wa.shape[1]
    q_end = 2 * GLA_HEADS * GLA_DK + GLA_HEADS * GLA_DV
    rank = p['w_gk2'].shape[1]
    w['w_in_a'] = jnp.concatenate(
        [wa[:, :, :q_end], wa[:, :, q_end + rank:], wa[:, :, q_end:q_end + rank],
         jnp.zeros((n_a, d, LANES - rank), wa.dtype)], axis=-1).astype(BF16)
    w['w_gk2'] = jnp.pad(p['w_gk2'], ((0, 0), (0, LANES - rank), (0, 0)))
    gw = p['lru_gate_w']
    nb, bw = gw.shape[2], gw.shape[3]
    eye = jnp.eye(nb, dtype=gw.dtype)
    dense = jnp.einsum('agncd,nm->agncmd', gw, eye).reshape(n_a, 2, nb * bw, nb * bw)
    w['lru_gate_w'] = jnp.concatenate([dense[:, 0], dense[:, 1]], axis=-1).astype(BF16)
    w['lru_gate_b'] = p['lru_gate_b'].reshape(n_a, 1, -1)
    w['w_out_a'] = p['w_out_a'].astype(BF16)
    wc = p['w_in_c']
    n_c = wc.shape[0]
    w['w_in_c'] = jnp.concatenate([wc, jnp.zeros((n_c, d, LANES - FOX_HEADS), wc.dtype)], axis=-1).astype(BF16)
    w['b_f'] = jnp.pad(p['b_f'], ((0, 0), (0, LANES - FOX_HEADS))).reshape(n_c, 1, LANES)
    w['w_out_c'] = p['w_out_c'].astype(BF16)
    w['ffn_w_up'] = p['ffn_w_up'].astype(BF16)
    w['ffn_w_down'] = p['ffn_w_down'].astype(BF16)
    return w


def _trunk(x, gla_s, lru_buf, lru_h, ffn_buf, attend, p, w, s, tiles):
    b, t, d = x.shape
    depth = p['norm_g'].shape[0]
    l_proj, l_gla, l_lru, l_ffn = (_pow2_tile(t, n) for n in tiles)
    chunk = np.gcd(t, GLA_CHUNK).item()
    a_widths = [GLA_HEADS * GLA_DK] * 2 + [GLA_HEADS * GLA_DV] * 2 + [lru_buf.shape[-1]] * 2 + [LANES]
    w_split = GLA_HEADS * GLA_DV
    gla_l, buf_l, h_l, k_l, v_l, lf_l, ffn_l = [], [], [], [], [], [], []
    ia = ic = 0
    for layer in range(depth):
        ng = p['norm_g'][layer].reshape(4, 1, d)
        if layer % 2 == 0:
            q, k, v, gg, xb, yb, lo = _norm_proj(x, ng[0], w['w_in_a'][ia], a_widths, s, l_proj)
            s0t = gla_s[:, ia].reshape(b, GLA_HEADS * GLA_DK, GLA_DV).transpose(0, 2, 1)
            o, snt = _gla(q, k, v, lo, gg, w['w_gk2'][ia], p['b_gk2'][ia].reshape(1, -1),
                          p['gla_norm_g'][ia].reshape(1, -1), s0t, s, l_gla, chunk)
            lru_o, nbuf, nh = _lru(xb, yb, lru_buf[:, ia], lru_h[:, ia][:, None, :], p['lru_conv_w'][ia],
                                   p['lru_conv_b'][ia].reshape(1, -1), w['lru_gate_w'][ia], w['lru_gate_b'][ia],
                                   p['lru_lambda'][ia].reshape(1, -1), s, l_lru)
            x = _out_proj([o, lru_o], [w['w_out_a'][ia][:w_split], w['w_out_a'][ia][w_split:]], x, ng[1], s, l_proj)
            gla_l.append(snt.transpose(0, 2, 1).reshape(b, GLA_HEADS, GLA_DK, GLA_DV))
            buf_l.append(nbuf)
            h_l.append(nh[:, 0, :])
            ia += 1
        else:
            qb, k, v, kb, vb, lf, c = _proj_c(x, ng[0], w['w_in_c'][ic], w['b_f'][ic], s, l_proj)
            o = attend(qb, k, v, kb, vb, c, ic)
            x = _out_proj([o], [w['w_out_c'][ic]], x, ng[1], s, l_proj)
            k_l.append(k.reshape(b, t, FOX_HEADS, FOX_HD))
            v_l.append(v.reshape(b, t, FOX_HEADS, FOX_HD))
            lf_l.append(lf)
            ic += 1
        x, fb = _ffn(x, ffn_buf[:, layer], ng[2], ng[3], w['ffn_w_up'][layer], p['ffn_conv_w'][layer],
                     p['ffn_conv_b'][layer].reshape(1, -1), w['ffn_w_down'][layer], s, l_ffn)
        ffn_l.append(fb)
    return (x, jnp.stack(gla_l, axis=1), jnp.stack(buf_l, axis=1), jnp.stack(h_l, axis=1), jnp.stack(k_l, axis=1),
            jnp.stack(v_l, axis=1), jnp.stack(lf_l, axis=1), jnp.stack(ffn_l, axis=1))


PROMPT_TILES = (512, 512, 256, 256)
FOX_TQ = 512


def kernel(x_prompt, x_sample, state_gla, state_lru_conv, state_lru_h, cache_k, cache_v, cache_logf, state_ffn_conv, page_table, norm_g, w_in_a, w_gk2, b_gk2, gla_norm_g, lru_conv_w, lru_conv_b, lru_gate_w, lru_gate_b, lru_lambda, w_out_a, w_in_c, b_f, w_out_c, ffn_w_up, ffn_conv_w, ffn_conv_b, ffn_w_down):
    p = {'norm_g': norm_g, 'w_in_a': w_in_a, 'w_gk2': w_gk2, 'b_gk2': b_gk2, 'gla_norm_g': gla_norm_g,
         'lru_conv_w': lru_conv_w, 'lru_conv_b': lru_conv_b, 'lru_gate_w': lru_gate_w, 'lru_gate_b': lru_gate_b,
         'lru_lambda': lru_lambda, 'w_out_a': w_out_a, 'w_in_c': w_in_c, 'b_f': b_f, 'w_out_c': w_out_c,
         'ffn_w_up': ffn_w_up, 'ffn_conv_w': ffn_conv_w, 'ffn_conv_b': ffn_conv_b, 'ffn_w_down': ffn_w_down}
    w = _prep_weights(p)
    bp, tp, _ = x_prompt.shape
    bs, ts, _ = x_sample.shape
    dt = x_prompt.dtype
    n_a, n_c, depth = w_in_a.shape[0], w_in_c.shape[0], norm_g.shape[0]
    gla0 = jnp.zeros((bp,) + state_gla.shape[1:], dt)
    lru_buf0 = jnp.zeros((bp,) + state_lru_conv.shape[1:], dt)
    lru_h0 = jnp.zeros((bp,) + state_lru_h.shape[1:], dt)
    ffn_buf0 = jnp.zeros((bp,) + state_ffn_conv.shape[1:], dt)

    def attend_prompt(qb, k, v, kb, vb, c, layer):
        return _fox_prompt(qb, kb, vb, c, _pow2_tile(tp, FOX_TQ))

    def attend_sample(qb, k, v, kb, vb, c, layer):
        return _fox_sample(qb, k, v, c, cache_k, cache_v, cache_logf, page_table, layer)

    outs_p = _trunk(x_prompt, gla0, lru_buf0, lru_h0, ffn_buf0, attend_prompt, p, w, 1, PROMPT_TILES)
    outs_s = _trunk(x_sample, state_gla, state_lru_conv, state_lru_h, state_ffn_conv, attend_sample, p, w, bs,
                    (ts, ts, ts, ts))
    return tuple(o for pair in zip(outs_p, outs_s) for o in pair)
```

```python
import functools

import numpy as np
import jax
import jax.numpy as jnp
from jax import lax
from jax.experimental import pallas as pl
from jax.experimental.pallas import tpu as pltpu

F32 = jnp.float32
BF16 = jnp.bfloat16
HIGHEST = lax.Precision.HIGHEST

EPS = 1e-6
GLA_HEADS = 4
GLA_DK = 64
GLA_DV = 128
GLA_GATE_NORM = 16.0
GLA_CHUNK = 64
LRU_C = 8.0
FOX_HEADS = 16
FOX_HD = 64
FOX_SCALE = FOX_HD ** -0.5
PAGE_SIZE = 128
LANES = 128
SUBLANES = 8
HIST = SUBLANES
VMEM_LIMIT = 56 * 2 ** 20
PAGES_PER_STEP = 4
NEG_INF = float("-inf")
LOG2E = 1.4426950408889634


def _rms(x, g):
    return x * lax.rsqrt(jnp.mean(x * x, axis=-1, keepdims=True) + EPS) * g


def _sigmoid(x):
    return 1.0 / (1.0 + jnp.exp(-x))


def _softplus(x):
    return jnp.maximum(x, 0.0) + jnp.log1p(jnp.exp(-jnp.abs(x)))


def _log_sigmoid(x):
    return -_softplus(-x)


def _expm1(x):
    u = jnp.exp(x)
    corrected = (jnp.abs(x) < 0.5) & (u != 1.0)
    ratio = x / jnp.log(jnp.where(corrected, u, 2.0))
    return jnp.where(u == 1.0, x, (u - 1.0) * jnp.where(corrected, ratio, 1.0))


def _gelu(x):
    return x * (0.5 * (1.0 + jnp.tanh(0.7978845608028654 * (x + 0.044715 * (x * x * x)))))


def _dot(a, b, precision=None):
    return jnp.dot(a, b, preferred_element_type=F32, precision=precision)


def _dot_nt(a, b, precision=None):
    return lax.dot_general(a, b, (((1,), (1,)), ((), ())), preferred_element_type=F32, precision=precision)


def _dot_tn(a, b, precision=None):
    return lax.dot_general(a, b, (((0,), (0,)), ((), ())), preferred_element_type=F32, precision=precision)


def _seq_tril(m, l):
    row = lax.broadcasted_iota(jnp.int32, (m, m), 0)
    col = lax.broadcasted_iota(jnp.int32, (m, m), 1)
    same = (row & -l) == (col & -l)
    return jnp.where(same & (col <= row), 1.0, 0.0).astype(F32)


def _split3(x):
    hi = x.astype(BF16).astype(F32)
    mid = (x - hi).astype(BF16).astype(F32)
    lo = (x - hi - mid).astype(BF16).astype(F32)
    return hi, mid, lo


def _params(sem):
    return pltpu.CompilerParams(dimension_semantics=sem, vmem_limit_bytes=VMEM_LIMIT)


def _tile_spec(s, l, n):
    return pl.BlockSpec((s, l, n), lambda b, t: (b, t, 0))


def _seq_spec(s, r, n):
    return pl.BlockSpec((s, r, n), lambda b, t: (b, 0, 0))


def _const_spec(shape):
    return pl.BlockSpec(shape, lambda b, t: (0,) * len(shape))


def _norm_proj_body(x_ref, g_ref, w_ref, *o_refs):
    s, l, d = x_ref.shape
    h = _rms(x_ref[...].reshape(s * l, d), g_ref[...]).astype(BF16)
    off = 0
    for o_ref in o_refs:
        n = o_ref.shape[-1]
        o_ref[...] = _dot(h, w_ref[:, off:off + n]).reshape(s, l, n).astype(o_ref.dtype)
        off += n


def _norm_proj(x, g, w, widths, s, l):
    b, t, d = x.shape
    return pl.pallas_call(
        _norm_proj_body,
        grid=(b // s, t // l),
        in_specs=[_tile_spec(s, l, d), _const_spec((1, d)), _const_spec(w.shape)],
        out_specs=[_tile_spec(s, l, n) for n in widths],
        out_shape=[jax.ShapeDtypeStruct((b, t, n), F32) for n in widths],
        compiler_params=_params(("parallel", "parallel")),
        name="norm_proj_a",
    )(x, g, w)


def _proj_cs_body(x_ref, g_ref, w_ref, bf_ref, q_ref, k_ref, v_ref, lf_ref, c_ref):
    s, l, d = x_ref.shape
    m = s * l
    da = q_ref.shape[-1]
    nh = lf_ref.shape[-1]
    h = _rms(x_ref[...].reshape(m, d), g_ref[...]).astype(BF16)
    q_ref[...] = _dot(h, w_ref[:, 0:da]).reshape(s, l, da)
    k_ref[...] = _dot(h, w_ref[:, da:2 * da]).reshape(s, l, da)
    v_ref[...] = _dot(h, w_ref[:, 2 * da:3 * da]).reshape(s, l, da)
    lf = _log_sigmoid(_dot(h, w_ref[:, 3 * da:]) + bf_ref[...])
    c = _dot(_seq_tril(m, l), lf, HIGHEST)
    lf_ref[...] = lf[:, :nh].reshape(s, l, nh)
    c_ref[...] = c[:, :nh].reshape(s, l, nh)


def _proj_c_sample(x, g, w, bf):
    b, t, d = x.shape
    da = FOX_HEADS * FOX_HD
    wide = jax.ShapeDtypeStruct((b, t, da), F32)
    narrow = jax.ShapeDtypeStruct((b, t, FOX_HEADS), F32)
    return pl.pallas_call(
        _proj_cs_body,
        grid=(1, 1),
        in_specs=[_tile_spec(b, t, d), _const_spec((1, d)), _const_spec(w.shape), _const_spec((1, LANES))],
        out_specs=[_tile_spec(b, t, da)] * 3 + [_tile_spec(b, t, FOX_HEADS)] * 2,
        out_shape=[wide, wide, wide, narrow, narrow],
        compiler_params=_params(("parallel", "arbitrary")),
        name="norm_proj_c_sample",
    )(x, g, w, bf)


def _proj_cp_body(x_ref, g_ref, wt_ref, wk_ref, bfc_ref, eye_ref,
                  qt_ref, kt_ref, vt_ref, lft_ref, ct2_ref, kaug_ref, vaug_ref, carry_ref):
    _, l, d = x_ref.shape
    da = kt_ref.shape[1]
    h = _rms(x_ref[0], g_ref[...]).astype(BF16)
    allt = _dot_nt(wt_ref[...], h)
    qt_ref[0] = (allt[0:da] * (FOX_SCALE * LOG2E)).astype(qt_ref.dtype)
    kt_ref[0] = allt[da:2 * da]
    vt = allt[2 * da:3 * da]
    vt_ref[0] = vt
    lft = _log_sigmoid(allt[3 * da:] + bfc_ref[...])
    lft_ref[0] = lft[:FOX_HEADS]

    @pl.when(pl.program_id(1) == 0)
    def _():
        carry_ref[...] = jnp.zeros_like(carry_ref)

    row = lax.broadcasted_iota(jnp.int32, (l, l), 0)
    col = lax.broadcasted_iota(jnp.int32, (l, l), 1)
    ct = _dot(lft, jnp.where(row <= col, 1.0, 0.0).astype(F32), HIGHEST) + carry_ref[...]
    carry_ref[...] = ct[:, l - 1:l]
    ct2 = ct[:FOX_HEADS] * LOG2E
    ct2_ref[0] = ct2
    hi, mid, lo = _split3(ct2)
    ones_rows = jnp.where(lax.broadcasted_iota(jnp.int32, (FOX_HEADS, l), 0) < 3, 1.0, 0.0)
    augt = jnp.concatenate([-hi, -mid, -lo, ones_rows, jnp.zeros((LANES - 4 * FOX_HEADS, l), F32)], axis=0).astype(BF16)
    aug = _dot_tn(augt, eye_ref[...]).astype(BF16)
    kb = _dot(h, wk_ref[...]).astype(BF16)
    pair = 2 * FOX_HD
    for p in range(FOX_HEADS // 2):
        kaug_ref[0, p] = jnp.concatenate([kb[:, p * pair:(p + 1) * pair], aug], axis=1)
    vtb = vt.astype(BF16)
    ones = jnp.ones((FOX_HD, l), BF16)
    for hh in range(FOX_HEADS):
        vh = vtb[hh * FOX_HD:(hh + 1) * FOX_HD]
        vaug_ref[0, hh, 0] = jnp.concatenate([vh, ones] if hh % 2 == 0 else [ones, vh], axis=0)


def _proj_c_prompt(x, g, wt, wk, bfc, eye, l):
    b, t, d = x.shape
    da = FOX_HEADS * FOX_HD
    nk = t // l
    pair = 2 * FOX_HD
    feat_t = pl.BlockSpec((1, da, l), lambda b_, t_: (b_, 0, t_))
    head_t = pl.BlockSpec((1, FOX_HEADS, l), lambda b_, t_: (b_, 0, t_))
    return pl.pallas_call(
        _proj_cp_body,
        grid=(b, nk),
        in_specs=[_tile_spec(1, l, d), _const_spec((1, d)), _const_spec(wt.shape), _const_spec(wk.shape),
                  _const_spec((LANES, 1)), _const_spec((LANES, LANES))],
        out_specs=[feat_t, feat_t, feat_t, head_t, head_t,
                   pl.BlockSpec((1, FOX_HEADS // 2, l, pair + LANES), lambda b_, t_: (b_, 0, t_, 0)),
                   pl.BlockSpec((1, FOX_HEADS, 1, pair, l), lambda b_, t_: (b_, 0, t_, 0, 0))],
        out_shape=[jax.ShapeDtypeStruct((b, da, t), BF16),
                   jax.ShapeDtypeStruct((b, da, t), F32),
                   jax.ShapeDtypeStruct((b, da, t), F32),
                   jax.ShapeDtypeStruct((b, FOX_HEADS, t), F32),
                   jax.ShapeDtypeStruct((b, FOX_HEADS, t), F32),
                   jax.ShapeDtypeStruct((b, FOX_HEADS // 2, t, pair + LANES), BF16),
                   jax.ShapeDtypeStruct((b, FOX_HEADS, nk, pair, l), BF16)],
        scratch_shapes=[pltpu.VMEM((LANES, 1), F32)],
        compiler_params=_params(("parallel", "arbitrary")),
        name="norm_proj_c_prompt",
    )(x, g, wt, wk, bfc, eye)


def _gla_body(q_ref, k_ref, v_ref, lo_ref, gg_ref, wgk_ref, bgk_ref, ng_ref, s0_ref, o_ref, sn_ref, st_ref, *, chunk):
    s, l, dq = q_ref.shape
    nc = l // chunk
    mm = BF16 if chunk >= 16 else F32
    t = pl.program_id(1)

    @pl.when(t == 0)
    def _():
        st_ref[...] = s0_ref[...]

    tri = _seq_tril(chunk, chunk)
    lane = lax.broadcasted_iota(jnp.int32, (1, dq), 1)
    wgk = wgk_ref[...].astype(mm)
    bgk = bgk_ref[...]
    ng = ng_ref[...]

    def do_chunk(si, rows):
        qi = q_ref[si, rows, :] * (GLA_DK ** -0.5)
        ki = k_ref[si, rows, :]
        vi = v_ref[si, rows, :]
        gi = gg_ref[si, rows, :]
        lo = lo_ref[si, rows, :]
        la = _log_sigmoid(_dot(lo.astype(mm), wgk) + bgk) / GLA_GATE_NORM
        bcum = _dot(tri, la, HIGHEST)
        blast = bcum[chunk - 1:chunk, :]
        qd = qi * jnp.exp(bcum)
        kd = (ki * jnp.exp(-bcum)).astype(mm)
        kd2 = (ki * jnp.exp(blast - bcum)).astype(mm)
        st = st_ref[si]
        stm = st.astype(mm)
        upd = jnp.zeros_like(st)
        for h in range(GLA_HEADS):
            hm = (lane >= h * GLA_DK) & (lane < (h + 1) * GLA_DK)
            qm = jnp.where(hm, qd, 0.0).astype(mm)
            att = jnp.where(tri > 0.0, _dot_nt(qm, kd), 0.0)
            vh = vi[:, h * GLA_DV:(h + 1) * GLA_DV].astype(mm)
            o = _dot(att.astype(mm), vh) + _dot_nt(qm, stm)
            upd = jnp.where(hm, _dot_tn(vh, kd2), upd)
            gh = gi[:, h * GLA_DV:(h + 1) * GLA_DV]
            o_ref[si, rows, h * GLA_DV:(h + 1) * GLA_DV] = (_rms(o, ng) * (gh * _sigmoid(gh))).astype(o_ref.dtype)
        st_ref[si] = st * jnp.exp(blast) + upd

    def do_seq(si, carry):
        if nc == 1:
            do_chunk(si, slice(0, chunk))
        else:
            def body(ci, c2):
                do_chunk(si, pl.ds(pl.multiple_of(ci * chunk, chunk), chunk))
                return c2
            lax.fori_loop(0, nc, body, 0)
        return carry

    if s == 1:
        do_seq(0, 0)
    else:
        lax.fori_loop(0, s, do_seq, 0)

    @pl.when(t == pl.num_programs(1) - 1)
    def _():
        sn_ref[...] = st_ref[...]


def _gla(q, k, v, lo, gg, wgk, bgk, ng, s0t, s, l, chunk):
    b, t, dq = q.shape
    dv = v.shape[-1]
    return pl.pallas_call(
        functools.partial(_gla_body, chunk=chunk),
        grid=(b // s, t // l),
        in_specs=[_tile_spec(s, l, dq), _tile_spec(s, l, dq), _tile_spec(s, l, dv), _tile_spec(s, l, LANES),
                  _tile_spec(s, l, dv), _const_spec(wgk.shape), _const_spec((1, dq)), _const_spec((1, GLA_DV)),
                  _seq_spec(s, GLA_DV, dq)],
        out_specs=[_tile_spec(s, l, dv), _seq_spec(s, GLA_DV, dq)],
        out_shape=[jax.ShapeDtypeStruct((b, t, dv), F32), jax.ShapeDtypeStruct((b, GLA_DV, dq), F32)],
        scratch_shapes=[pltpu.VMEM((s, GLA_DV, dq), F32)],
        compiler_params=_params(("parallel", "arbitrary")),
        name="gla",
    )(q, k, v, lo, gg, wgk, bgk, ng, s0t)


def _lru_body(xb_ref, yb_ref, cbuf_ref, h0_ref, cw_ref, cb_ref, gw_ref, gb_ref, lam_ref,
              o_ref, ncbuf_ref, nh_ref, xbuf, hc):
    s, l, c = xb_ref.shape
    m = s * l
    nw = cw_ref.shape[0]
    t = pl.program_id(1)

    @pl.when(t == 0)
    def _():
        xbuf[:, HIST - (nw - 1):HIST, :] = cbuf_ref[...]
        hc[...] = h0_ref[...]

    @pl.when(t > 0)
    def _():
        xbuf[:, 0:HIST, :] = xbuf[:, l:l + HIST, :]

    xbuf[:, HIST:HIST + l, :] = xb_ref[...]
    xc = cb_ref[...].reshape(1, 1, c)
    for j in range(nw):
        off = HIST - (nw - 1) + j
        xc = xc + xbuf[:, off:off + l, :] * cw_ref[j:j + 1, :].reshape(1, 1, c)
    xc = xc.reshape(m, c)
    gates = _dot(xc.astype(BF16), gw_ref[...]) + gb_ref[...]
    r = _sigmoid(gates[:, :c])
    i = _sigmoid(gates[:, c:])
    log_a = (-LRU_C * r) * _softplus(-lam_ref[...])
    a = jnp.exp(log_a)
    bx = jnp.sqrt(-_expm1(2.0 * log_a)) * i * xc
    pos = lax.broadcasted_iota(jnp.int32, (m, c), 0) & (l - 1)
    d = 1
    while d < l:
        valid = pos >= d
        a_prev = jnp.where(valid, pltpu.roll(a, d, 0), 1.0)
        b_prev = jnp.where(valid, pltpu.roll(bx, d, 0), 0.0)
        bx = a * b_prev + bx
        a = a * a_prev
        d *= 2
    hin = jnp.broadcast_to(hc[...], (s, l, c)).reshape(m, c)
    hs = bx + a * hin
    o_ref[...] = (hs * _gelu(yb_ref[...].reshape(m, c))).reshape(s, l, c)
    hc[...] = hs.reshape(s, l, c)[:, l - 1:l, :]

    @pl.when(t == pl.num_programs(1) - 1)
    def _():
        ncbuf_ref[...] = xbuf[:, l + HIST - (nw - 1):l + HIST, :]
        nh_ref[...] = hc[...]


def _lru(xb, yb, cbuf, h0, cw, cb, gw, gb, lam, s, l):
    b, t, c = xb.shape
    nw = cw.shape[0]
    return pl.pallas_call(
        _lru_body,
        grid=(b // s, t // l),
        in_specs=[_tile_spec(s, l, c), _tile_spec(s, l, c), _seq_spec(s, nw - 1, c), _seq_spec(s, 1, c),
                  _const_spec(cw.shape), _const_spec((1, c)), _const_spec(gw.shape), _const_spec((1, 2 * c)),
                  _const_spec((1, c))],
        out_specs=[_tile_spec(s, l, c), _seq_spec(s, nw - 1, c), _seq_spec(s, 1, c)],
        out_shape=[jax.ShapeDtypeStruct((b, t, c), F32), jax.ShapeDtypeStruct((b, nw - 1, c), F32),
                   jax.ShapeDtypeStruct((b, 1, c), F32)],
        scratch_shapes=[pltpu.VMEM((s, l + HIST, c), F32), pltpu.VMEM((s, 1, c), F32)],
        compiler_params=_params(("parallel", "arbitrary")),
        name="rg_lru",
    )(xb, yb, cbuf, h0, cw, cb, gw, gb, lam)


def _out_proj_body(*refs, n_in):
    a_refs, w_refs = refs[:n_in], refs[n_in:2 * n_in]
    x_ref, g_ref, y_ref = refs[2 * n_in:]
    s, l, d = x_ref.shape
    m = s * l
    acc = None
    for a_ref, w_ref in zip(a_refs, w_refs):
        part = _dot(a_ref[...].reshape(m, a_ref.shape[-1]).astype(BF16), w_ref[...])
        acc = part if acc is None else acc + part
    y_ref[...] = (x_ref[...].reshape(m, d) + _rms(acc, g_ref[...])).reshape(s, l, d)


def _out_proj(acts, ws, x, g, s, l):
    b, t, d = x.shape
    n_in = len(acts)
    return pl.pallas_call(
        functools.partial(_out_proj_body, n_in=n_in),
        grid=(b // s, t // l),
        in_specs=[_tile_spec(s, l, a.shape[-1]) for a in acts] + [_const_spec(w.shape) for w in ws]
                 + [_tile_spec(s, l, d), _const_spec((1, d))],
        out_specs=_tile_spec(s, l, d),
        out_shape=jax.ShapeDtypeStruct((b, t, d), F32),
        compiler_params=_params(("parallel", "parallel")),
        name="out_proj",
    )(*acts, *ws, x, g)


def _ffn_body(x_ref, buf_ref, g2_ref, g3_ref, wup_ref, cw_ref, cb_ref, wdn_ref, y_ref, nbuf_ref, gbuf):
    s, l, d = x_ref.shape
    m = s * l
    nw, f = cw_ref.shape
    t = pl.program_id(1)
    x = x_ref[...].reshape(m, d)
    h = _rms(x, g2_ref[...]).astype(BF16)

    @pl.when(t == 0)
    def _():
        gbuf[:, HIST - (nw - 1):HIST, :] = buf_ref[...]

    @pl.when(t > 0)
    def _():
        gbuf[:, 0:HIST, :] = gbuf[:, l:l + HIST, :]

    gbuf[:, HIST:HIST + l, :] = _dot(h, wup_ref[:, :f]).reshape(s, l, f)
    u = _dot(h, wup_ref[:, f:])
    gc = cb_ref[...].reshape(1, 1, f)
    for j in range(nw):
        off = HIST - (nw - 1) + j
        gc = gc + gbuf[:, off:off + l, :] * cw_ref[j:j + 1, :].reshape(1, 1, f)
    act = (_gelu(gc).reshape(m, f) * u).astype(BF16)
    y = _dot(act, wdn_ref[...])
    y_ref[...] = (x + _rms(y, g3_ref[...])).reshape(s, l, d)

    @pl.when(t == pl.num_programs(1) - 1)
    def _():
        nbuf_ref[...] = gbuf[:, l + HIST - (nw - 1):l + HIST, :]


def _ffn(x, buf, g2, g3, wup, cw, cb, wdn, s, l):
    b, t, d = x.shape
    nw, f = cw.shape
    return pl.pallas_call(
        _ffn_body,
        grid=(b // s, t // l),
        in_specs=[_tile_spec(s, l, d), _seq_spec(s, nw - 1, f), _const_spec((1, d)), _const_spec((1, d)),
                  pl.BlockSpec(wup.shape, lambda b_, t_: (0, 0), pipeline_mode=pl.Buffered(1)),
                  _const_spec(cw.shape), _const_spec((1, f)),
                  pl.BlockSpec(wdn.shape, lambda b_, t_: (0, 0), pipeline_mode=pl.Buffered(1))],
        out_specs=[_tile_spec(s, l, d), _seq_spec(s, nw - 1, f)],
        out_shape=[jax.ShapeDtypeStruct((b, t, d), F32), jax.ShapeDtypeStruct((b, nw - 1, f), F32)],
        scratch_shapes=[pltpu.VMEM((s, l + HIST, f), F32)],
        compiler_params=_params(("parallel", "arbitrary")),
        name="conv_ffn",
    )(x, buf, g2, g3, wup, cw, cb, wdn)


def _fox_prompt_body(qt_ref, ct_ref, kaug_ref, vaug_ref, o_ref, m_ref, acc_ref, *, tq):
    hp = pl.program_id(1)
    iq = pl.program_id(2)
    qt = qt_ref[0]
    feat = lax.broadcasted_iota(jnp.int32, (LANES, 1), 0)
    qas = []
    for j in range(2):
        head = 2 * hp + j
        hi, mid, lo = _split3(ct_ref[0, pl.ds(head, 1), :])
        pick = jnp.where(((feat & (FOX_HEADS - 1)) == head) & (feat < 3 * FOX_HEADS), 1.0, 0.0)
        augt = jnp.where(feat == 3 * FOX_HEADS, hi,
                         jnp.where(feat == 3 * FOX_HEADS + 1, mid, jnp.where(feat == 3 * FOX_HEADS + 2, lo, pick)))
        own = (feat < FOX_HD) if j == 0 else (feat >= FOX_HD)
        qas.append(jnp.concatenate([jnp.where(own, qt, jnp.zeros_like(qt)), augt.astype(BF16)], axis=0))
    qa = jnp.concatenate(qas, axis=1)
    m_ref[...] = jnp.full_like(m_ref, NEG_INF)
    acc_ref[...] = jnp.zeros_like(acc_ref)
    key = lax.broadcasted_iota(jnp.int32, (tq, tq), 0)
    qry = lax.broadcasted_iota(jnp.int32, (tq, tq), 1)

    def step(ik, masked):
        kt = kaug_ref[0, 0, pl.ds(pl.multiple_of(ik * tq, tq), tq), :]
        s2 = _dot(kt, qa)
        for j in range(2):
            st = s2[:, j * tq:(j + 1) * tq]
            if masked:
                st = jnp.where(key <= qry, st, NEG_INF)
            m_old = m_ref[j]
            m_new = jnp.maximum(m_old, jnp.max(st, axis=0, keepdims=True))
            pt = jnp.exp2(st - m_new).astype(BF16)
            m_ref[j] = m_new
            acc_ref[j] = acc_ref[j] * jnp.exp2(m_old - m_new) + _dot(vaug_ref[0, j, ik], pt)

    def loop_body(ik, carry):
        step(ik, False)
        return carry

    lax.fori_loop(0, iq, loop_body, 0)
    step(iq, True)
    a0 = acc_ref[0]
    a1 = acc_ref[1]
    ot = jnp.concatenate([a0[:FOX_HD] / a0[FOX_HD:], a1[FOX_HD:] / a1[:FOX_HD]], axis=0)
    o_ref[0] = ot.T.astype(o_ref.dtype)


def _fox_prompt(qt, ct2, kaug, vaug, tq):
    b, da, t = qt.shape
    nq = t // tq
    assert vaug.shape[2] == nq and vaug.shape[-1] == tq
    return pl.pallas_call(
        functools.partial(_fox_prompt_body, tq=tq),
        grid=(b, FOX_HEADS // 2, nq),
        in_specs=[pl.BlockSpec((1, LANES, tq), lambda b_, h_, i_: (b_, h_, i_)),
                  pl.BlockSpec((1, FOX_HEADS, tq), lambda b_, h_, i_: (b_, 0, i_)),
                  pl.BlockSpec((1, 1) + kaug.shape[2:], lambda b_, h_, i_: (b_, h_, 0, 0)),
                  pl.BlockSpec((1, 2) + vaug.shape[2:], lambda b_, h_, i_: (b_, h_, 0, 0, 0))],
        out_specs=pl.BlockSpec((1, tq, LANES), lambda b_, h_, i_: (b_, i_, h_)),
        out_shape=jax.ShapeDtypeStruct((b, t, da), BF16),
        scratch_shapes=[pltpu.VMEM((2, 1, tq), F32), pltpu.VMEM((2, LANES, tq), F32)],
        compiler_params=_params(("parallel", "parallel", "arbitrary")),
        name="fox_prompt",
    )(qt, ct2, kaug, vaug)


def _fox_sample_body(pt_ref, q_ref, kn_ref, vn_ref, cnt_ref, cnr_ref, *rest, npp):
    kp = rest[0:npp]
    vp = rest[npp:2 * npp]
    lp = rest[2 * npp:3 * npp]
    o_ref, m_ref, l_ref, acc_ref, carry_ref, kb_ref, vb_ref = rest[3 * npp:]
    del pt_ref
    j = pl.program_id(1)
    t, da = q_ref.shape[1], q_ref.shape[2]
    rows = FOX_HEADS * t
    row_c = lax.broadcasted_iota(jnp.int32, (rows, da), 0)
    col_c = lax.broadcasted_iota(jnp.int32, (rows, da), 1)
    own = (col_c & -FOX_HD) == (row_c & -t) * (FOX_HD // t)
    q8 = q_ref[0] * FOX_SCALE
    qbd = jnp.where(own, jnp.concatenate([q8] * FOX_HEADS, axis=0), 0.0).astype(BF16)
    cn_rows = cnr_ref[0]

    def head_rows(x):
        return jnp.concatenate([jnp.broadcast_to(x[h:h + 1, :], (t, x.shape[1])) for h in range(FOX_HEADS)], axis=0)

    @pl.when(j == 0)
    def _():
        pad = jnp.zeros((PAGE_SIZE - t, da), F32)
        kn = jnp.concatenate([kn_ref[0], pad], axis=0).astype(BF16)
        vn = jnp.concatenate([vn_ref[0], pad], axis=0).astype(BF16)
        row_l = lax.broadcasted_iota(jnp.int32, (rows, PAGE_SIZE), 0)
        col_l = lax.broadcasted_iota(jnp.int32, (rows, PAGE_SIZE), 1)
        sc = _dot_nt(qbd, kn) + cn_rows - head_rows(cnt_ref[0])
        sc = jnp.where(col_l <= (row_l & (t - 1)), sc, NEG_INF)
        m0 = jnp.max(sc, axis=-1, keepdims=True)
        p = jnp.exp(sc - m0)
        m_ref[...] = m0
        l_ref[...] = jnp.sum(p, axis=-1, keepdims=True)
        acc_ref[...] = _dot(p.astype(BF16), vn)
        carry_ref[...] = jnp.zeros_like(carry_ref)

    prow = lax.broadcasted_iota(jnp.int32, (PAGE_SIZE, PAGE_SIZE), 0)
    pcol = lax.broadcasted_iota(jnp.int32, (PAGE_SIZE, PAGE_SIZE), 1)
    later = jnp.where(prow > pcol, 1.0, 0.0).astype(F32)
    biases = []
    for i in range(npp):
        lft = lp[i][...]
        suffix = _dot(lft, later, HIGHEST) + carry_ref[...]
        carry_ref[...] = suffix[:, 0:1] + lft[:, 0:1]
        biases.append(head_rows(suffix))
        kb_ref[:, i * PAGE_SIZE:(i + 1) * PAGE_SIZE] = kp[i][...].astype(BF16)
        vb_ref[:, i * PAGE_SIZE:(i + 1) * PAGE_SIZE] = vp[i][...].astype(BF16)
    sc = _dot(qbd, kb_ref[...]) + cn_rows + jnp.concatenate(biases, axis=1)
    m_old = m_ref[...]
    m_new = jnp.maximum(m_old, jnp.max(sc, axis=-1, keepdims=True))
    alpha = jnp.exp(m_old - m_new)
    p = jnp.exp(sc - m_new)
    l_new = alpha * l_ref[...] + jnp.sum(p, axis=-1, keepdims=True)
    acc = acc_ref[...] * alpha + _dot_nt(p.astype(BF16), vb_ref[...])
    m_ref[...] = m_new
    l_ref[...] = l_new
    acc_ref[...] = acc

    @pl.when(j == pl.num_programs(1) - 1)
    def _():
        full = jnp.where(own, acc / l_new, 0.0)
        out = full[0:t, :]
        for h in range(1, FOX_HEADS):
            out = out + full[h * t:(h + 1) * t, :]
        o_ref[0] = out


def _fox_sample(q, k, v, c, cache_k, cache_v, cache_logf, page_table, layer):
    b, t, da = q.shape
    n_pages = page_table.shape[1]
    npp = PAGES_PER_STEP
    assert n_pages % npp == 0 and PAGE_SIZE % t == 0 and t % SUBLANES == 0
    n_pool, n_layers = cache_k.shape[0], cache_k.shape[1]
    ckt = cache_k.transpose(0, 1, 3, 4, 2).reshape(n_pool, n_layers, da, PAGE_SIZE)
    cvt = cache_v.transpose(0, 1, 3, 4, 2).reshape(n_pool, n_layers, da, PAGE_SIZE)
    clt = cache_logf.transpose(0, 1, 3, 2)
    ct = c.transpose(0, 2, 1)
    cn_t = jnp.pad(ct, ((0, 0), (0, 0), (0, PAGE_SIZE - t)))
    cn_rows = ct.reshape(b, FOX_HEADS * t, 1)

    def page_spec(i, height):
        return pl.BlockSpec((None, None, height, PAGE_SIZE),
                            lambda b_, j_, pt: (pt[b_, n_pages - 1 - (j_ * npp + i)], layer, 0, 0))

    tok = pl.BlockSpec((1, t, da), lambda b_, j_, pt: (b_, 0, 0))
    rows = FOX_HEADS * t
    grid_spec = pltpu.PrefetchScalarGridSpec(
        num_scalar_prefetch=1,
        grid=(b, n_pages // npp),
        in_specs=[tok, tok, tok,
                  pl.BlockSpec((1, FOX_HEADS, PAGE_SIZE), lambda b_, j_, pt: (b_, 0, 0)),
                  pl.BlockSpec((1, rows, 1), lambda b_, j_, pt: (b_, 0, 0))]
                 + [page_spec(i, da) for i in range(npp)] * 2
                 + [page_spec(i, FOX_HEADS) for i in range(npp)],
        out_specs=tok,
        scratch_shapes=[pltpu.VMEM((rows, 1), F32), pltpu.VMEM((rows, 1), F32), pltpu.VMEM((rows, da), F32),
                        pltpu.VMEM((FOX_HEADS, 1), F32),
                        pltpu.VMEM((da, npp * PAGE_SIZE), BF16), pltpu.VMEM((da, npp * PAGE_SIZE), BF16)],
    )
    return pl.pallas_call(
        functools.partial(_fox_sample_body, npp=npp),
        grid_spec=grid_spec,
        out_shape=jax.ShapeDtypeStruct((b, t, da), F32),
        compiler_params=_params(("parallel", "arbitrary")),
        name="fox_sample",
    )(page_table, q, k, v, cn_t, cn_rows, *([ckt] * npp), *([cvt] * npp), *([clt] * npp))


def _pow2_tile(t, want):
    l = min(t, want)
    assert t % l == 0 and l & (l - 1) == 0
    return l


def _prep_weights(p):
    w = {}
    n_a = p['w_in_a'].shape[0]
    wa = p['w_in_a']
    d = wa.shape[1]
    q_end = 2 * GLA_HEADS * GLA_DK + GLA_HEADS * GLA_DV
    rank = p['w_gk2'].shape[1]
    w['w_in_a'] = jnp.concatenate(
        [wa[:, :, :q_end], wa[:, :, q_end + rank:], wa[:, :, q_end:q_end + rank],
         jnp.zeros((n_a, d, LANES - rank), wa.dtype)], axis=-1).astype(BF16)
    w['w_gk2'] = jnp.pad(p['w_gk2'], ((0, 0), (0, LANES - rank), (0, 0)))
    gw = p['lru_gate_w']
    nb, bw = gw.shape[2], gw.shape[3]
    eye = jnp.eye(nb, dtype=gw.dtype)
    dense = jnp.einsum('agncd,nm->agncmd', gw, eye).reshape(n_a, 2, nb * bw, nb * bw)
    w['lru_gate_w'] = jnp.concatenate([dense[:, 0], dense[:, 1]], axis=-1).astype(BF16)
    w['lru_gate_b'] = p['lru_gate_b'].reshape(n_a, 1, -1)
    w['w_out_a'] = p['w_out_a'].astype(BF16)
    wc = p['w_in_c']
    n_c = wc.shape[0]
    da = FOX_HEADS * FOX_HD
    w['w_in_c'] = jnp.concatenate([wc, jnp.zeros((n_c, d, LANES - FOX_HEADS), wc.dtype)], axis=-1).astype(BF16)
    w['b_f'] = jnp.pad(p['b_f'], ((0, 0), (0, LANES - FOX_HEADS))).reshape(n_c, 1, LANES)
    w['w_qkvf_t'] = jnp.swapaxes(w['w_in_c'], 1, 2)
    w['w_k_c'] = w['w_in_c'][:, :, da:2 * da]
    w['b_f_col'] = w['b_f'].reshape(n_c, LANES, 1)
    w['eye'] = jnp.eye(LANES, dtype=BF16)
    w['w_out_c'] = p['w_out_c'].astype(BF16)
    w['ffn_w_up'] = p['ffn_w_up'].astype(BF16)
    w['ffn_w_down'] = p['ffn_w_down'].astype(BF16)
    return w


def _trunk(x, gla_s, lru_buf, lru_h, ffn_buf, layer_c, p, w, s, tiles):
    b, t, d = x.shape
    depth = p['norm_g'].shape[0]
    l_proj, l_gla, l_lru, l_ffn = (_pow2_tile(t, n) for n in tiles)
    chunk = np.gcd(t, GLA_CHUNK).item()
    a_widths = [GLA_HEADS * GLA_DK] * 2 + [GLA_HEADS * GLA_DV] * 2 + [lru_buf.shape[-1]] * 2 + [LANES]
    w_split = GLA_HEADS * GLA_DV
    gla_l, buf_l, h_l, k_l, v_l, lf_l, ffn_l = [], [], [], [], [], [], []
    ia = ic = 0
    for layer in range(depth):
        ng = p['norm_g'][layer].reshape(4, 1, d)
        if layer % 2 == 0:
            q, k, v, gg, xb, yb, lo = _norm_proj(x, ng[0], w['w_in_a'][ia], a_widths, s, l_proj)
            s0t = gla_s[:, ia].reshape(b, GLA_HEADS * GLA_DK, GLA_DV).transpose(0, 2, 1)
            o, snt = _gla(q, k, v, lo, gg, w['w_gk2'][ia], p['b_gk2'][ia].reshape(1, -1),
                          p['gla_norm_g'][ia].reshape(1, -1), s0t, s, l_gla, chunk)
            lru_o, nbuf, nh = _lru(xb, yb, lru_buf[:, ia], lru_h[:, ia][:, None, :], p['lru_conv_w'][ia],
                                   p['lru_conv_b'][ia].reshape(1, -1), w['lru_gate_w'][ia], w['lru_gate_b'][ia],
                                   p['lru_lambda'][ia].reshape(1, -1), s, l_lru)
            x = _out_proj([o, lru_o], [w['w_out_a'][ia][:w_split], w['w_out_a'][ia][w_split:]], x, ng[1], s, l_proj)
            gla_l.append(snt.transpose(0, 2, 1).reshape(b, GLA_HEADS, GLA_DK, GLA_DV))
            buf_l.append(nbuf)
            h_l.append(nh[:, 0, :])
            ia += 1
        else:
            o, k_new, v_new, lf_new = layer_c(x, ng[0], ic)
            x = _out_proj([o], [w['w_out_c'][ic]], x, ng[1], s, l_proj)
            k_l.append(k_new)
            v_l.append(v_new)
            lf_l.append(lf_new)
            ic += 1
        x, fb = _ffn(x, ffn_buf[:, layer], ng[2], ng[3], w['ffn_w_up'][layer], p['ffn_conv_w'][layer],
                     p['ffn_conv_b'][layer].reshape(1, -1), w['ffn_w_down'][layer], s, l_ffn)
        ffn_l.append(fb)
    return (x, jnp.stack(gla_l, axis=1), jnp.stack(buf_l, axis=1), jnp.stack(h_l, axis=1), jnp.stack(k_l, axis=1),
            jnp.stack(v_l, axis=1), jnp.stack(lf_l, axis=1), jnp.stack(ffn_l, axis=1))


PROMPT_TILES = (512, 512, 256, 256)
FOX_TQ = 512


def kernel(x_prompt, x_sample, state_gla, state_lru_conv, state_lru_h, cache_k, cache_v, cache_logf, state_ffn_conv, page_table, norm_g, w_in_a, w_gk2, b_gk2, gla_norm_g, lru_conv_w, lru_conv_b, lru_gate_w, lru_gate_b, lru_lambda, w_out_a, w_in_c, b_f, w_out_c, ffn_w_up, ffn_conv_w, ffn_conv_b, ffn_w_down):
    p = {'norm_g': norm_g, 'w_in_a': w_in_a, 'w_gk2': w_gk2, 'b_gk2': b_gk2, 'gla_norm_g': gla_norm_g,
         'lru_conv_w': lru_conv_w, 'lru_conv_b': lru_conv_b, 'lru_gate_w': lru_gate_w, 'lru_gate_b': lru_gate_b,
         'lru_lambda': lru_lambda, 'w_out_a': w_out_a, 'w_in_c': w_in_c, 'b_f': b_f, 'w_out_c': w_out_c,
         'ffn_w_up': ffn_w_up, 'ffn_conv_w': ffn_conv_w, 'ffn_conv_b': ffn_conv_b, 'ffn_w_down': ffn_w_down}
    w = _prep_weights(p)
    bp = x_prompt.shape[0]
    bs, ts, _ = x_sample.shape
    dt = x_prompt.dtype
    gla0 = jnp.zeros((bp,) + state_gla.shape[1:], dt)
    lru_buf0 = jnp.zeros((bp,) + state_lru_conv.shape[1:], dt)
    lru_h0 = jnp.zeros((bp,) + state_lru_h.shape[1:], dt)
    ffn_buf0 = jnp.zeros((bp,) + state_ffn_conv.shape[1:], dt)

    def layer_c_prompt(x, g, ic):
        b, t, _ = x.shape
        l = _pow2_tile(t, FOX_TQ)
        qt, kt, vt, lft, ct2, kaug, vaug = _proj_c_prompt(x, g, w['w_qkvf_t'][ic], w['w_k_c'][ic], w['b_f_col'][ic],
                                                          w['eye'], l)
        o = _fox_prompt(qt, ct2, kaug, vaug, l)
        k_new = kt.reshape(b, FOX_HEADS, FOX_HD, t).transpose(0, 3, 1, 2)
        v_new = vt.reshape(b, FOX_HEADS, FOX_HD, t).transpose(0, 3, 1, 2)
        return o, k_new, v_new, lft.transpose(0, 2, 1)

    def layer_c_sample(x, g, ic):
        b, t, _ = x.shape
        q, k, v, lf, c = _proj_c_sample(x, g, w['w_in_c'][ic], w['b_f'][ic])
        o = _fox_sample(q, k, v, c, cache_k, cache_v, cache_logf, page_table, ic)
        return o, k.reshape(b, t, FOX_HEADS, FOX_HD), v.reshape(b, t, FOX_HEADS, FOX_HD), lf

    outs_p = _trunk(x_prompt, gla0, lru_buf0, lru_h0, ffn_buf0, layer_c_prompt, p, w, 1, PROMPT_TILES)
    outs_s = _trunk(x_sample, state_gla, state_lru_conv, state_lru_h, state_ffn_conv, layer_c_sample, p, w, bs,
                    (ts, ts, ts, ts))
    return tuple(o for pair in zip(outs_p, outs_s) for o in pair)
```

```python
import functools

import numpy as np
import jax
import jax.numpy as jnp
from jax import lax
from jax.experimental import pallas as pl
from jax.experimental.pallas import tpu as pltpu

F32 = jnp.float32
BF16 = jnp.bfloat16
HIGHEST = lax.Precision.HIGHEST

EPS = 1e-6
GLA_HEADS = 4
GLA_DK = 64
GLA_DV = 128
GLA_GATE_NORM = 16.0
GLA_CHUNK = 64
LRU_C = 8.0
FOX_HEADS = 16
FOX_HD = 64
FOX_SCALE = FOX_HD ** -0.5
PAGE_SIZE = 128
LANES = 128
SUBLANES = 8
HIST = SUBLANES
VMEM_LIMIT = 56 * 2 ** 20
PAGES_PER_STEP = 8
NEG_INF = float("-inf")
LOG2E = 1.4426950408889634


def _rms(x, g):
    return x * lax.rsqrt(jnp.mean(x * x, axis=-1, keepdims=True) + EPS) * g


def _sigmoid(x):
    return 1.0 / (1.0 + jnp.exp(-x))


def _softplus(x):
    return jnp.maximum(x, 0.0) + jnp.log1p(jnp.exp(-jnp.abs(x)))


def _log_sigmoid(x):
    return -_softplus(-x)


def _expm1(x):
    u = jnp.exp(x)
    corrected = (jnp.abs(x) < 0.5) & (u != 1.0)
    ratio = x / jnp.log(jnp.where(corrected, u, 2.0))
    return jnp.where(u == 1.0, x, (u - 1.0) * jnp.where(corrected, ratio, 1.0))


def _gelu(x):
    return x * (0.5 * (1.0 + jnp.tanh(0.7978845608028654 * (x + 0.044715 * (x * x * x)))))


def _dot(a, b, precision=None):
    return jnp.dot(a, b, preferred_element_type=F32, precision=precision)


def _dot_nt(a, b, precision=None):
    return lax.dot_general(a, b, (((1,), (1,)), ((), ())), preferred_element_type=F32, precision=precision)


def _dot_tn(a, b, precision=None):
    return lax.dot_general(a, b, (((0,), (0,)), ((), ())), preferred_element_type=F32, precision=precision)


def _seq_tril(m, l):
    row = lax.broadcasted_iota(jnp.int32, (m, m), 0)
    col = lax.broadcasted_iota(jnp.int32, (m, m), 1)
    same = (row & -l) == (col & -l)
    return jnp.where(same & (col <= row), 1.0, 0.0).astype(F32)


def _split3(x):
    hi = x.astype(BF16).astype(F32)
    mid = (x - hi).astype(BF16).astype(F32)
    lo = (x - hi - mid).astype(BF16).astype(F32)
    return hi, mid, lo


def _params(sem):
    return pltpu.CompilerParams(dimension_semantics=sem, vmem_limit_bytes=VMEM_LIMIT)


def _tile_spec(s, l, n):
    return pl.BlockSpec((s, l, n), lambda b, t: (b, t, 0))


def _seq_spec(s, r, n):
    return pl.BlockSpec((s, r, n), lambda b, t: (b, 0, 0))


def _const_spec(shape):
    return pl.BlockSpec(shape, lambda b, t: (0,) * len(shape))


def _norm_proj_body(x_ref, g_ref, w_ref, *o_refs):
    s, l, d = x_ref.shape
    h = _rms(x_ref[...].reshape(s * l, d), g_ref[...]).astype(BF16)
    off = 0
    for o_ref in o_refs:
        n = o_ref.shape[-1]
        o_ref[...] = _dot(h, w_ref[:, off:off + n]).reshape(s, l, n).astype(o_ref.dtype)
        off += n


def _norm_proj(x, g, w, widths, s, l):
    b, t, d = x.shape
    return pl.pallas_call(
        _norm_proj_body,
        grid=(b // s, t // l),
        in_specs=[_tile_spec(s, l, d), _const_spec((1, d)), _const_spec(w.shape)],
        out_specs=[_tile_spec(s, l, n) for n in widths],
        out_shape=[jax.ShapeDtypeStruct((b, t, n), F32) for n in widths],
        compiler_params=_params(("parallel", "parallel")),
        name="norm_proj_a",
    )(x, g, w)


def _proj_cs_body(x_ref, g_ref, w_ref, bf_ref, q_ref, k_ref, v_ref, lf_ref, c_ref):
    s, l, d = x_ref.shape
    m = s * l
    da = q_ref.shape[-1]
    nh = lf_ref.shape[-1]
    h = _rms(x_ref[...].reshape(m, d), g_ref[...]).astype(BF16)
    q_ref[...] = _dot(h, w_ref[:, 0:da]).reshape(s, l, da)
    k_ref[...] = _dot(h, w_ref[:, da:2 * da]).reshape(s, l, da)
    v_ref[...] = _dot(h, w_ref[:, 2 * da:3 * da]).reshape(s, l, da)
    lf = _log_sigmoid(_dot(h, w_ref[:, 3 * da:]) + bf_ref[...])
    c = _dot(_seq_tril(m, l), lf, HIGHEST)
    lf_ref[...] = lf[:, :nh].reshape(s, l, nh)
    c_ref[...] = c[:, :nh].reshape(s, l, nh)


def _proj_c_sample(x, g, w, bf):
    b, t, d = x.shape
    da = FOX_HEADS * FOX_HD
    wide = jax.ShapeDtypeStruct((b, t, da), F32)
    narrow = jax.ShapeDtypeStruct((b, t, FOX_HEADS), F32)
    return pl.pallas_call(
        _proj_cs_body,
        grid=(1, 1),
        in_specs=[_tile_spec(b, t, d), _const_spec((1, d)), _const_spec(w.shape), _const_spec((1, LANES))],
        out_specs=[_tile_spec(b, t, da)] * 3 + [_tile_spec(b, t, FOX_HEADS)] * 2,
        out_shape=[wide, wide, wide, narrow, narrow],
        compiler_params=_params(("parallel", "arbitrary")),
        name="norm_proj_c_sample",
    )(x, g, w, bf)


def _proj_cp_body(x_ref, g_ref, wt_ref, wk_ref, bfc_ref, eye_ref,
                  qt_ref, kt_ref, vt_ref, lft_ref, ct2_ref, kaug_ref, vaug_ref, carry_ref):
    _, l, d = x_ref.shape
    da = kt_ref.shape[1]
    h = _rms(x_ref[0], g_ref[...]).astype(BF16)
    allt = _dot_nt(wt_ref[...], h)
    qt_ref[0] = (allt[0:da] * (FOX_SCALE * LOG2E)).astype(qt_ref.dtype)
    kt_ref[0] = allt[da:2 * da]
    vt = allt[2 * da:3 * da]
    vt_ref[0] = vt
    lft = _log_sigmoid(allt[3 * da:] + bfc_ref[...])
    lft_ref[0] = lft[:FOX_HEADS]

    @pl.when(pl.program_id(1) == 0)
    def _():
        carry_ref[...] = jnp.zeros_like(carry_ref)

    row = lax.broadcasted_iota(jnp.int32, (l, l), 0)
    col = lax.broadcasted_iota(jnp.int32, (l, l), 1)
    ct = _dot(lft, jnp.where(row <= col, 1.0, 0.0).astype(F32), HIGHEST) + carry_ref[...]
    carry_ref[...] = ct[:, l - 1:l]
    ct2 = ct[:FOX_HEADS] * LOG2E
    ct2_ref[0] = ct2
    hi, mid, lo = _split3(ct2)
    ones_rows = jnp.where(lax.broadcasted_iota(jnp.int32, (FOX_HEADS, l), 0) < 3, 1.0, 0.0)
    augt = jnp.concatenate([-hi, -mid, -lo, ones_rows, jnp.zeros((LANES - 4 * FOX_HEADS, l), F32)], axis=0).astype(BF16)
    aug = _dot_tn(augt, eye_ref[...]).astype(BF16)
    kb = _dot(h, wk_ref[...]).astype(BF16)
    pair = 2 * FOX_HD
    for p in range(FOX_HEADS // 2):
        kaug_ref[0, p] = jnp.concatenate([kb[:, p * pair:(p + 1) * pair], aug], axis=1)
    vtb = vt.astype(BF16)
    ones = jnp.ones((FOX_HD, l), BF16)
    for hh in range(FOX_HEADS):
        vh = vtb[hh * FOX_HD:(hh + 1) * FOX_HD]
        vaug_ref[0, hh, 0] = jnp.concatenate([vh, ones] if hh % 2 == 0 else [ones, vh], axis=0)


def _proj_c_prompt(x, g, wt, wk, bfc, eye, l):
    b, t, d = x.shape
    da = FOX_HEADS * FOX_HD
    nk = t // l
    pair = 2 * FOX_HD
    feat_t = pl.BlockSpec((1, da, l), lambda b_, t_: (b_, 0, t_))
    head_t = pl.BlockSpec((1, FOX_HEADS, l), lambda b_, t_: (b_, 0, t_))
    return pl.pallas_call(
        _proj_cp_body,
        grid=(b, nk),
        in_specs=[_tile_spec(1, l, d), _const_spec((1, d)), _const_spec(wt.shape), _const_spec(wk.shape),
                  _const_spec((LANES, 1)), _const_spec((LANES, LANES))],
        out_specs=[feat_t, feat_t, feat_t, head_t, head_t,
                   pl.BlockSpec((1, FOX_HEADS // 2, l, pair + LANES), lambda b_, t_: (b_, 0, t_, 0)),
                   pl.BlockSpec((1, FOX_HEADS, 1, pair, l), lambda b_, t_: (b_, 0, t_, 0, 0))],
        out_shape=[jax.ShapeDtypeStruct((b, da, t), BF16),
                   jax.ShapeDtypeStruct((b, da, t), F32),
                   jax.ShapeDtypeStruct((b, da, t), F32),
                   jax.ShapeDtypeStruct((b, FOX_HEADS, t), F32),
                   jax.ShapeDtypeStruct((b, FOX_HEADS, t), F32),
                   jax.ShapeDtypeStruct((b, FOX_HEADS // 2, t, pair + LANES), BF16),
                   jax.ShapeDtypeStruct((b, FOX_HEADS, nk, pair, l), BF16)],
        scratch_shapes=[pltpu.VMEM((LANES, 1), F32)],
        compiler_params=_params(("parallel", "arbitrary")),
        name="norm_proj_c_prompt",
    )(x, g, wt, wk, bfc, eye)


def _gla_body(q_ref, k_ref, v_ref, lo_ref, gg_ref, wgk_ref, bgk_ref, ng_ref, s0_ref, o_ref, sn_ref, st_ref, *, chunk):
    s, l, dq = q_ref.shape
    nc = l // chunk
    mm = BF16 if chunk >= 16 else F32
    t = pl.program_id(1)

    @pl.when(t == 0)
    def _():
        st_ref[...] = s0_ref[...]

    tri = _seq_tril(chunk, chunk)
    lane = lax.broadcasted_iota(jnp.int32, (1, dq), 1)
    wgk = wgk_ref[...].astype(mm)
    bgk = bgk_ref[...]
    ng = ng_ref[...]

    def do_chunk(si, rows):
        qi = q_ref[si, rows, :] * (GLA_DK ** -0.5)
        ki = k_ref[si, rows, :]
        vi = v_ref[si, rows, :]
        gi = gg_ref[si, rows, :]
        lo = lo_ref[si, rows, :]
        la = _log_sigmoid(_dot(lo.astype(mm), wgk) + bgk) / GLA_GATE_NORM
        bcum = _dot(tri, la, HIGHEST)
        blast = bcum[chunk - 1:chunk, :]
        qd = qi * jnp.exp(bcum)
        kd = (ki * jnp.exp(-bcum)).astype(mm)
        kd2 = (ki * jnp.exp(blast - bcum)).astype(mm)
        st = st_ref[si]
        stm = st.astype(mm)
        upd = jnp.zeros_like(st)
        for h in range(GLA_HEADS):
            hm = (lane >= h * GLA_DK) & (lane < (h + 1) * GLA_DK)
            qm = jnp.where(hm, qd, 0.0).astype(mm)
            att = jnp.where(tri > 0.0, _dot_nt(qm, kd), 0.0)
            vh = vi[:, h * GLA_DV:(h + 1) * GLA_DV].astype(mm)
            o = _dot(att.astype(mm), vh) + _dot_nt(qm, stm)
            upd = jnp.where(hm, _dot_tn(vh, kd2), upd)
            gh = gi[:, h * GLA_DV:(h + 1) * GLA_DV]
            o_ref[si, rows, h * GLA_DV:(h + 1) * GLA_DV] = (_rms(o, ng) * (gh * _sigmoid(gh))).astype(o_ref.dtype)
        st_ref[si] = st * jnp.exp(blast) + upd

    def do_seq(si, carry):
        if nc == 1:
            do_chunk(si, slice(0, chunk))
        else:
            def body(ci, c2):
                do_chunk(si, pl.ds(pl.multiple_of(ci * chunk, chunk), chunk))
                return c2
            lax.fori_loop(0, nc, body, 0)
        return carry

    if s == 1:
        do_seq(0, 0)
    else:
        lax.fori_loop(0, s, do_seq, 0)

    @pl.when(t == pl.num_programs(1) - 1)
    def _():
        sn_ref[...] = st_ref[...]


def _gla(q, k, v, lo, gg, wgk, bgk, ng, s0t, s, l, chunk):
    b, t, dq = q.shape
    dv = v.shape[-1]
    return pl.pallas_call(
        functools.partial(_gla_body, chunk=chunk),
        grid=(b // s, t // l),
        in_specs=[_tile_spec(s, l, dq), _tile_spec(s, l, dq), _tile_spec(s, l, dv), _tile_spec(s, l, LANES),
                  _tile_spec(s, l, dv), _const_spec(wgk.shape), _const_spec((1, dq)), _const_spec((1, GLA_DV)),
                  _seq_spec(s, GLA_DV, dq)],
        out_specs=[_tile_spec(s, l, dv), _seq_spec(s, GLA_DV, dq)],
        out_shape=[jax.ShapeDtypeStruct((b, t, dv), F32), jax.ShapeDtypeStruct((b, GLA_DV, dq), F32)],
        scratch_shapes=[pltpu.VMEM((s, GLA_DV, dq), F32)],
        compiler_params=_params(("parallel", "arbitrary")),
        name="gla",
    )(q, k, v, lo, gg, wgk, bgk, ng, s0t)


def _lru_body(xb_ref, yb_ref, cbuf_ref, h0_ref, cw_ref, cb_ref, gw_ref, gb_ref, lam_ref,
              o_ref, ncbuf_ref, nh_ref, xbuf, hc):
    s, l, c = xb_ref.shape
    m = s * l
    nw = cw_ref.shape[0]
    t = pl.program_id(1)

    @pl.when(t == 0)
    def _():
        xbuf[:, HIST - (nw - 1):HIST, :] = cbuf_ref[...]
        hc[...] = h0_ref[...]

    @pl.when(t > 0)
    def _():
        xbuf[:, 0:HIST, :] = xbuf[:, l:l + HIST, :]

    xbuf[:, HIST:HIST + l, :] = xb_ref[...]
    xc = cb_ref[...].reshape(1, 1, c)
    for j in range(nw):
        off = HIST - (nw - 1) + j
        xc = xc + xbuf[:, off:off + l, :] * cw_ref[j:j + 1, :].reshape(1, 1, c)
    xc = xc.reshape(m, c)
    gates = _dot(xc.astype(BF16), gw_ref[...]) + gb_ref[...]
    r = _sigmoid(gates[:, :c])
    i = _sigmoid(gates[:, c:])
    log_a = (-LRU_C * r) * _softplus(-lam_ref[...])
    a = jnp.exp(log_a)
    bx = jnp.sqrt(-_expm1(2.0 * log_a)) * i * xc
    pos = lax.broadcasted_iota(jnp.int32, (m, c), 0) & (l - 1)
    d = 1
    while d < l:
        valid = pos >= d
        a_prev = jnp.where(valid, pltpu.roll(a, d, 0), 1.0)
        b_prev = jnp.where(valid, pltpu.roll(bx, d, 0), 0.0)
        bx = a * b_prev + bx
        a = a * a_prev
        d *= 2
    hin = jnp.broadcast_to(hc[...], (s, l, c)).reshape(m, c)
    hs = bx + a * hin
    o_ref[...] = (hs * _gelu(yb_ref[...].reshape(m, c))).reshape(s, l, c)
    hc[...] = hs.reshape(s, l, c)[:, l - 1:l, :]

    @pl.when(t == pl.num_programs(1) - 1)
    def _():
        ncbuf_ref[...] = xbuf[:, l + HIST - (nw - 1):l + HIST, :]
        nh_ref[...] = hc[...]


def _lru(xb, yb, cbuf, h0, cw, cb, gw, gb, lam, s, l):
    b, t, c = xb.shape
    nw = cw.shape[0]
    return pl.pallas_call(
        _lru_body,
        grid=(b // s, t // l),
        in_specs=[_tile_spec(s, l, c), _tile_spec(s, l, c), _seq_spec(s, nw - 1, c), _seq_spec(s, 1, c),
                  _const_spec(cw.shape), _const_spec((1, c)), _const_spec(gw.shape), _const_spec((1, 2 * c)),
                  _const_spec((1, c))],
        out_specs=[_tile_spec(s, l, c), _seq_spec(s, nw - 1, c), _seq_spec(s, 1, c)],
        out_shape=[jax.ShapeDtypeStruct((b, t, c), F32), jax.ShapeDtypeStruct((b, nw - 1, c), F32),
                   jax.ShapeDtypeStruct((b, 1, c), F32)],
        scratch_shapes=[pltpu.VMEM((s, l + HIST, c), F32), pltpu.VMEM((s, 1, c), F32)],
        compiler_params=_params(("parallel", "arbitrary")),
        name="rg_lru",
    )(xb, yb, cbuf, h0, cw, cb, gw, gb, lam)


def _out_proj_body(*refs, n_in):
    a_refs, w_refs = refs[:n_in], refs[n_in:2 * n_in]
    x_ref, g_ref, y_ref = refs[2 * n_in:]
    s, l, d = x_ref.shape
    m = s * l
    acc = None
    for a_ref, w_ref in zip(a_refs, w_refs):
        part = _dot(a_ref[...].reshape(m, a_ref.shape[-1]).astype(BF16), w_ref[...])
        acc = part if acc is None else acc + part
    y_ref[...] = (x_ref[...].reshape(m, d) + _rms(acc, g_ref[...])).reshape(s, l, d)


def _out_proj(acts, ws, x, g, s, l):
    b, t, d = x.shape
    n_in = len(acts)
    return pl.pallas_call(
        functools.partial(_out_proj_body, n_in=n_in),
        grid=(b // s, t // l),
        in_specs=[_tile_spec(s, l, a.shape[-1]) for a in acts] + [_const_spec(w.shape) for w in ws]
                 + [_tile_spec(s, l, d), _const_spec((1, d))],
        out_specs=_tile_spec(s, l, d),
        out_shape=jax.ShapeDtypeStruct((b, t, d), F32),
        compiler_params=_params(("parallel", "parallel")),
        name="out_proj",
    )(*acts, *ws, x, g)


def _ffn_body(x_ref, buf_ref, g2_ref, g3_ref, wup_ref, cw_ref, cb_ref, wdn_ref, y_ref, nbuf_ref, gbuf):
    s, l, d = x_ref.shape
    m = s * l
    nw, f = cw_ref.shape
    t = pl.program_id(1)
    x = x_ref[...].reshape(m, d)
    h = _rms(x, g2_ref[...]).astype(BF16)

    @pl.when(t == 0)
    def _():
        gbuf[:, HIST - (nw - 1):HIST, :] = buf_ref[...]

    @pl.when(t > 0)
    def _():
        gbuf[:, 0:HIST, :] = gbuf[:, l:l + HIST, :]

    gbuf[:, HIST:HIST + l, :] = _dot(h, wup_ref[:, :f]).reshape(s, l, f)
    u = _dot(h, wup_ref[:, f:])
    gc = cb_ref[...].reshape(1, 1, f)
    for j in range(nw):
        off = HIST - (nw - 1) + j
        gc = gc + gbuf[:, off:off + l, :] * cw_ref[j:j + 1, :].reshape(1, 1, f)
    act = (_gelu(gc).reshape(m, f) * u).astype(BF16)
    y = _dot(act, wdn_ref[...])
    y_ref[...] = (x + _rms(y, g3_ref[...])).reshape(s, l, d)

    @pl.when(t == pl.num_programs(1) - 1)
    def _():
        nbuf_ref[...] = gbuf[:, l + HIST - (nw - 1):l + HIST, :]


def _ffn(x, buf, g2, g3, wup, cw, cb, wdn, s, l):
    b, t, d = x.shape
    nw, f = cw.shape
    return pl.pallas_call(
        _ffn_body,
        grid=(b // s, t // l),
        in_specs=[_tile_spec(s, l, d), _seq_spec(s, nw - 1, f), _const_spec((1, d)), _const_spec((1, d)),
                  pl.BlockSpec(wup.shape, lambda b_, t_: (0, 0), pipeline_mode=pl.Buffered(1)),
                  _const_spec(cw.shape), _const_spec((1, f)),
                  pl.BlockSpec(wdn.shape, lambda b_, t_: (0, 0), pipeline_mode=pl.Buffered(1))],
        out_specs=[_tile_spec(s, l, d), _seq_spec(s, nw - 1, f)],
        out_shape=[jax.ShapeDtypeStruct((b, t, d), F32), jax.ShapeDtypeStruct((b, nw - 1, f), F32)],
        scratch_shapes=[pltpu.VMEM((s, l + HIST, f), F32)],
        compiler_params=_params(("parallel", "arbitrary")),
        name="conv_ffn",
    )(x, buf, g2, g3, wup, cw, cb, wdn)


def _fox_prompt_body(qt_ref, ct_ref, kaug_ref, vaug_ref, o_ref, m_ref, acc_ref, sa_ref, sb_ref, *, tq):
    hp = pl.program_id(1)
    iq = pl.program_id(2)
    qt = qt_ref[0]
    feat = lax.broadcasted_iota(jnp.int32, (LANES, 1), 0)
    qas = []
    for j in range(2):
        head = 2 * hp + j
        hi, mid, lo = _split3(ct_ref[0, pl.ds(head, 1), :])
        pick = jnp.where(((feat & (FOX_HEADS - 1)) == head) & (feat < 3 * FOX_HEADS), 1.0, 0.0)
        augt = jnp.where(feat == 3 * FOX_HEADS, hi,
                         jnp.where(feat == 3 * FOX_HEADS + 1, mid, jnp.where(feat == 3 * FOX_HEADS + 2, lo, pick)))
        own = (feat < FOX_HD) if j == 0 else (feat >= FOX_HD)
        qas.append(jnp.concatenate([jnp.where(own, qt, jnp.zeros_like(qt)), augt.astype(BF16)], axis=0))
    qa = jnp.concatenate(qas, axis=1)
    m_ref[...] = jnp.full_like(m_ref, NEG_INF)
    acc_ref[...] = jnp.zeros_like(acc_ref)
    key = lax.broadcasted_iota(jnp.int32, (tq, tq), 0)
    qry = lax.broadcasted_iota(jnp.int32, (tq, tq), 1)

    def logits(ik, s_ref):
        kt = kaug_ref[0, 0, pl.ds(pl.multiple_of(ik * tq, tq), tq), :]
        s_ref[...] = _dot(kt, qa)

    def update(ik, s_ref, masked):
        for j in range(2):
            st = s_ref[:, j * tq:(j + 1) * tq]
            if masked:
                st = jnp.where(key <= qry, st, NEG_INF)
            m_old = m_ref[j]
            m_new = jnp.maximum(m_old, jnp.max(st, axis=0, keepdims=True))
            pt = jnp.exp2(st - m_new).astype(BF16)
            m_ref[j] = m_new
            acc_ref[j] = acc_ref[j] * jnp.exp2(m_old - m_new) + _dot(vaug_ref[0, j, ik], pt)

    logits(0, sa_ref)

    def pair_body(i, carry):
        logits(2 * i + 1, sb_ref)
        update(2 * i, sa_ref, False)
        logits(2 * i + 2, sa_ref)
        update(2 * i + 1, sb_ref, False)
        return carry

    lax.fori_loop(0, iq >> 1, pair_body, 0)

    @pl.when((iq & 1) == 0)
    def _():
        update(iq, sa_ref, True)

    @pl.when((iq & 1) == 1)
    def _():
        logits(iq, sb_ref)
        update(iq - 1, sa_ref, False)
        update(iq, sb_ref, True)

    a0 = acc_ref[0]
    a1 = acc_ref[1]
    ot = jnp.concatenate([a0[:FOX_HD] / a0[FOX_HD:], a1[FOX_HD:] / a1[:FOX_HD]], axis=0)
    o_ref[0] = ot.T.astype(o_ref.dtype)


def _fox_prompt(qt, ct2, kaug, vaug, tq):
    b, da, t = qt.shape
    nq = t // tq
    assert vaug.shape[2] == nq and vaug.shape[-1] == tq
    return pl.pallas_call(
        functools.partial(_fox_prompt_body, tq=tq),
        grid=(b, FOX_HEADS // 2, nq),
        in_specs=[pl.BlockSpec((1, LANES, tq), lambda b_, h_, i_: (b_, h_, i_)),
                  pl.BlockSpec((1, FOX_HEADS, tq), lambda b_, h_, i_: (b_, 0, i_)),
                  pl.BlockSpec((1, 1) + kaug.shape[2:], lambda b_, h_, i_: (b_, h_, 0, 0)),
                  pl.BlockSpec((1, 2) + vaug.shape[2:], lambda b_, h_, i_: (b_, h_, 0, 0, 0))],
        out_specs=pl.BlockSpec((1, tq, LANES), lambda b_, h_, i_: (b_, i_, h_)),
        out_shape=jax.ShapeDtypeStruct((b, t, da), BF16),
        scratch_shapes=[pltpu.VMEM((2, 1, tq), F32), pltpu.VMEM((2, LANES, tq), F32),
                        pltpu.VMEM((tq, 2 * tq), F32), pltpu.VMEM((tq, 2 * tq), F32)],
        compiler_params=_params(("parallel", "parallel", "arbitrary")),
        name="fox_prompt",
    )(qt, ct2, kaug, vaug)


def _fox_sample_body(pt_ref, q_ref, kn_ref, vn_ref, cnt_ref, cnr_ref, *rest, npp):
    kp = rest[0:npp]
    vp = rest[npp:2 * npp]
    lp = rest[2 * npp:3 * npp]
    o_ref, m_ref, l_ref, acc_ref, carry_ref, kb_ref, vb_ref = rest[3 * npp:]
    del pt_ref
    j = pl.program_id(1)
    t, da = q_ref.shape[1], q_ref.shape[2]
    rows = FOX_HEADS * t
    row_c = lax.broadcasted_iota(jnp.int32, (rows, da), 0)
    col_c = lax.broadcasted_iota(jnp.int32, (rows, da), 1)
    own = (col_c & -FOX_HD) == (row_c & -t) * (FOX_HD // t)
    q8 = q_ref[0] * FOX_SCALE
    qbd = jnp.where(own, jnp.concatenate([q8] * FOX_HEADS, axis=0), 0.0).astype(BF16)
    cn_rows = cnr_ref[0]

    def head_rows(x):
        return jnp.concatenate([jnp.broadcast_to(x[h:h + 1, :], (t, x.shape[1])) for h in range(FOX_HEADS)], axis=0)

    @pl.when(j == 0)
    def _():
        pad = jnp.zeros((PAGE_SIZE - t, da), F32)
        kn = jnp.concatenate([kn_ref[0], pad], axis=0).astype(BF16)
        vn = jnp.concatenate([vn_ref[0], pad], axis=0).astype(BF16)
        row_l = lax.broadcasted_iota(jnp.int32, (rows, PAGE_SIZE), 0)
        col_l = lax.broadcasted_iota(jnp.int32, (rows, PAGE_SIZE), 1)
        sc = _dot_nt(qbd, kn) + cn_rows - head_rows(cnt_ref[0])
        sc = jnp.where(col_l <= (row_l & (t - 1)), sc, NEG_INF)
        m0 = jnp.max(sc, axis=-1, keepdims=True)
        p = jnp.exp(sc - m0)
        m_ref[...] = m0
        l_ref[...] = jnp.sum(p, axis=-1, keepdims=True)
        acc_ref[...] = _dot(p.astype(BF16), vn)
        carry_ref[...] = jnp.zeros_like(carry_ref)

    prow = lax.broadcasted_iota(jnp.int32, (PAGE_SIZE, PAGE_SIZE), 0)
    pcol = lax.broadcasted_iota(jnp.int32, (PAGE_SIZE, PAGE_SIZE), 1)
    later = jnp.where(prow > pcol, 1.0, 0.0).astype(F32)
    for i in range(npp):
        kb_ref[:, i * PAGE_SIZE:(i + 1) * PAGE_SIZE] = kp[i][...].astype(BF16)
        vb_ref[:, i * PAGE_SIZE:(i + 1) * PAGE_SIZE] = vp[i][...].astype(BF16)
    lf_all = jnp.concatenate([lp[i][...] for i in range(npp)], axis=0)
    inside = _dot(lf_all, later, HIGHEST)
    biases = []
    carry = carry_ref[...]
    for i in range(npp):
        rows_i = slice(i * FOX_HEADS, (i + 1) * FOX_HEADS)
        biases.append(head_rows(inside[rows_i] + carry))
        carry = carry + (inside[rows_i, 0:1] + lf_all[rows_i, 0:1])
    carry_ref[...] = carry
    sc = _dot(qbd, kb_ref[...]) + cn_rows + jnp.concatenate(biases, axis=1)
    m_old = m_ref[...]
    m_new = jnp.maximum(m_old, jnp.max(sc, axis=-1, keepdims=True))
    alpha = jnp.exp(m_old - m_new)
    p = jnp.exp(sc - m_new)
    l_new = alpha * l_ref[...] + jnp.sum(p, axis=-1, keepdims=True)
    acc = acc_ref[...] * alpha + _dot_nt(p.astype(BF16), vb_ref[...])
    m_ref[...] = m_new
    l_ref[...] = l_new
    acc_ref[...] = acc

    @pl.when(j == pl.num_programs(1) - 1)
    def _():
        full = jnp.where(own, acc / l_new, 0.0)
        out = full[0:t, :]
        for h in range(1, FOX_HEADS):
            out = out + full[h * t:(h + 1) * t, :]
        o_ref[0] = out


def _fox_sample(q, k, v, c, cache_k, cache_v, cache_logf, page_table, layer):
    b, t, da = q.shape
    n_pages = page_table.shape[1]
    npp = PAGES_PER_STEP
    assert n_pages % npp == 0 and PAGE_SIZE % t == 0 and t % SUBLANES == 0
    n_pool, n_layers = cache_k.shape[0], cache_k.shape[1]
    ckt = cache_k.transpose(0, 1, 3, 4, 2).reshape(n_pool, n_layers, da, PAGE_SIZE)
    cvt = cache_v.transpose(0, 1, 3, 4, 2).reshape(n_pool, n_layers, da, PAGE_SIZE)
    clt = cache_logf.transpose(0, 1, 3, 2)
    ct = c.transpose(0, 2, 1)
    cn_t = jnp.pad(ct, ((0, 0), (0, 0), (0, PAGE_SIZE - t)))
    cn_rows = ct.reshape(b, FOX_HEADS * t, 1)

    def page_spec(i, height):
        return pl.BlockSpec((None, None, height, PAGE_SIZE),
                            lambda b_, j_, pt: (pt[b_, n_pages - 1 - (j_ * npp + i)], layer, 0, 0))

    tok = pl.BlockSpec((1, t, da), lambda b_, j_, pt: (b_, 0, 0))
    rows = FOX_HEADS * t
    grid_spec = pltpu.PrefetchScalarGridSpec(
        num_scalar_prefetch=1,
        grid=(b, n_pages // npp),
        in_specs=[tok, tok, tok,
                  pl.BlockSpec((1, FOX_HEADS, PAGE_SIZE), lambda b_, j_, pt: (b_, 0, 0)),
                  pl.BlockSpec((1, rows, 1), lambda b_, j_, pt: (b_, 0, 0))]
                 + [page_spec(i, da) for i in range(npp)] * 2
                 + [page_spec(i, FOX_HEADS) for i in range(npp)],
        out_specs=tok,
        scratch_shapes=[pltpu.VMEM((rows, 1), F32), pltpu.VMEM((rows, 1), F32), pltpu.VMEM((rows, da), F32),
                        pltpu.VMEM((FOX_HEADS, 1), F32),
                        pltpu.VMEM((da, npp * PAGE_SIZE), BF16), pltpu.VMEM((da, npp * PAGE_SIZE), BF16)],
    )
    return pl.pallas_call(
        functools.partial(_fox_sample_body, npp=npp),
        grid_spec=grid_spec,
        out_shape=jax.ShapeDtypeStruct((b, t, da), F32),
        compiler_params=_params(("parallel", "arbitrary")),
        name="fox_sample",
    )(page_table, q, k, v, cn_t, cn_rows, *([ckt] * npp), *([cvt] * npp), *([clt] * npp))


def _pow2_tile(t, want):
    l = min(t, want)
    assert t % l == 0 and l & (l - 1) == 0
    return l


def _prep_weights(p):
    w = {}
    n_a = p['w_in_a'].shape[0]
    wa = p['w_in_a']
    d = wa.shape[1]
    q_end = 2 * GLA_HEADS * GLA_DK + GLA_HEADS * GLA_DV
    rank = p['w_gk2'].shape[1]
    w['w_in_a'] = jnp.concatenate(
        [wa[:, :, :q_end], wa[:, :, q_end + rank:], wa[:, :, q_end:q_end + rank],
         jnp.zeros((n_a, d, LANES - rank), wa.dtype)], axis=-1).astype(BF16)
    w['w_gk2'] = jnp.pad(p['w_gk2'], ((0, 0), (0, LANES - rank), (0, 0)))
    gw = p['lru_gate_w']
    nb, bw = gw.shape[2], gw.shape[3]
    eye = jnp.eye(nb, dtype=gw.dtype)
    dense = jnp.einsum('agncd,nm->agncmd', gw, eye).reshape(n_a, 2, nb * bw, nb * bw)
    w['lru_gate_w'] = jnp.concatenate([dense[:, 0], dense[:, 1]], axis=-1).astype(BF16)
    w['lru_gate_b'] = p['lru_gate_b'].reshape(n_a, 1, -1)
    w['w_out_a'] = p['w_out_a'].astype(BF16)
    wc = p['w_in_c']
    n_c = wc.shape[0]
    da = FOX_HEADS * FOX_HD
    w['w_in_c'] = jnp.concatenate([wc, jnp.zeros((n_c, d, LANES - FOX_HEADS), wc.dtype)], axis=-1).astype(BF16)
    w['b_f'] = jnp.pad(p['b_f'], ((0, 0), (0, LANES - FOX_HEADS))).reshape(n_c, 1, LANES)
    w['w_qkvf_t'] = jnp.swapaxes(w['w_in_c'], 1, 2)
    w['w_k_c'] = w['w_in_c'][:, :, da:2 * da]
    w['b_f_col'] = w['b_f'].reshape(n_c, LANES, 1)
    w['eye'] = jnp.eye(LANES, dtype=BF16)
    w['w_out_c'] = p['w_out_c'].astype(BF16)
    w['ffn_w_up'] = p['ffn_w_up'].astype(BF16)
    w['ffn_w_down'] = p['ffn_w_down'].astype(BF16)
    return w


def _trunk(x, gla_s, lru_buf, lru_h, ffn_buf, layer_c, p, w, s, tiles):
    b, t, d = x.shape
    depth = p['norm_g'].shape[0]
    l_proj, l_gla, l_lru, l_ffn = (_pow2_tile(t, n) for n in tiles)
    chunk = np.gcd(t, GLA_CHUNK).item()
    a_widths = [GLA_HEADS * GLA_DK] * 2 + [GLA_HEADS * GLA_DV] * 2 + [lru_buf.shape[-1]] * 2 + [LANES]
    w_split = GLA_HEADS * GLA_DV
    gla_l, buf_l, h_l, k_l, v_l, lf_l, ffn_l = [], [], [], [], [], [], []
    ia = ic = 0
    for layer in range(depth):
        ng = p['norm_g'][layer].reshape(4, 1, d)
        if layer % 2 == 0:
            q, k, v, gg, xb, yb, lo = _norm_proj(x, ng[0], w['w_in_a'][ia], a_widths, s, l_proj)
            s0t = gla_s[:, ia].reshape(b, GLA_HEADS * GLA_DK, GLA_DV).transpose(0, 2, 1)
            o, snt = _gla(q, k, v, lo, gg, w['w_gk2'][ia], p['b_gk2'][ia].reshape(1, -1),
                          p['gla_norm_g'][ia].reshape(1, -1), s0t, s, l_gla, chunk)
            lru_o, nbuf, nh = _lru(xb, yb, lru_buf[:, ia], lru_h[:, ia][:, None, :], p['lru_conv_w'][ia],
                                   p['lru_conv_b'][ia].reshape(1, -1), w['lru_gate_w'][ia], w['lru_gate_b'][ia],
                                   p['lru_lambda'][ia].reshape(1, -1), s, l_lru)
            x = _out_proj([o, lru_o], [w['w_out_a'][ia][:w_split], w['w_out_a'][ia][w_split:]], x, ng[1], s, l_proj)
            gla_l.append(snt.transpose(0, 2, 1).reshape(b, GLA_HEADS, GLA_DK, GLA_DV))
            buf_l.append(nbuf)
            h_l.append(nh[:, 0, :])
            ia += 1
        else:
            o, k_new, v_new, lf_new = layer_c(x, ng[0], ic)
            x = _out_proj([o], [w['w_out_c'][ic]], x, ng[1], s, l_proj)
            k_l.append(k_new)
            v_l.append(v_new)
            lf_l.append(lf_new)
            ic += 1
        x, fb = _ffn(x, ffn_buf[:, layer], ng[2], ng[3], w['ffn_w_up'][layer], p['ffn_conv_w'][layer],
                     p['ffn_conv_b'][layer].reshape(1, -1), w['ffn_w_down'][layer], s, l_ffn)
        ffn_l.append(fb)
    return (x, jnp.stack(gla_l, axis=1), jnp.stack(buf_l, axis=1), jnp.stack(h_l, axis=1), jnp.stack(k_l, axis=1),
            jnp.stack(v_l, axis=1), jnp.stack(lf_l, axis=1), jnp.stack(ffn_l, axis=1))


PROMPT_TILES = (512, 512, 256, 256)
FOX_TQ = 512


def kernel(x_prompt, x_sample, state_gla, state_lru_conv, state_lru_h, cache_k, cache_v, cache_logf, state_ffn_conv, page_table, norm_g, w_in_a, w_gk2, b_gk2, gla_norm_g, lru_conv_w, lru_conv_b, lru_gate_w, lru_gate_b, lru_lambda, w_out_a, w_in_c, b_f, w_out_c, ffn_w_up, ffn_conv_w, ffn_conv_b, ffn_w_down):
    p = {'norm_g': norm_g, 'w_in_a': w_in_a, 'w_gk2': w_gk2, 'b_gk2': b_gk2, 'gla_norm_g': gla_norm_g,
         'lru_conv_w': lru_conv_w, 'lru_conv_b': lru_conv_b, 'lru_gate_w': lru_gate_w, 'lru_gate_b': lru_gate_b,
         'lru_lambda': lru_lambda, 'w_out_a': w_out_a, 'w_in_c': w_in_c, 'b_f': b_f, 'w_out_c': w_out_c,
         'ffn_w_up': ffn_w_up, 'ffn_conv_w': ffn_conv_w, 'ffn_conv_b': ffn_conv_b, 'ffn_w_down': ffn_w_down}
    w = _prep_weights(p)
    bp = x_prompt.shape[0]
    bs, ts, _ = x_sample.shape
    dt = x_prompt.dtype
    gla0 = jnp.zeros((bp,) + state_gla.shape[1:], dt)
    lru_buf0 = jnp.zeros((bp,) + state_lru_conv.shape[1:], dt)
    lru_h0 = jnp.zeros((bp,) + state_lru_h.shape[1:], dt)
    ffn_buf0 = jnp.zeros((bp,) + state_ffn_conv.shape[1:], dt)

    def layer_c_prompt(x, g, ic):
        b, t, _ = x.shape
        l = _pow2_tile(t, FOX_TQ)
        qt, kt, vt, lft, ct2, kaug, vaug = _proj_c_prompt(x, g, w['w_qkvf_t'][ic], w['w_k_c'][ic], w['b_f_col'][ic],
                                                          w['eye'], l)
        o = _fox_prompt(qt, ct2, kaug, vaug, l)
        k_new = kt.reshape(b, FOX_HEADS, FOX_HD, t).transpose(0, 3, 1, 2)
        v_new = vt.reshape(b, FOX_HEADS, FOX_HD, t).transpose(0, 3, 1, 2)
        return o, k_new, v_new, lft.transpose(0, 2, 1)

    def layer_c_sample(x, g, ic):
        b, t, _ = x.shape
        q, k, v, lf, c = _proj_c_sample(x, g, w['w_in_c'][ic], w['b_f'][ic])
        o = _fox_sample(q, k, v, c, cache_k, cache_v, cache_logf, page_table, ic)
        return o, k.reshape(b, t, FOX_HEADS, FOX_HD), v.reshape(b, t, FOX_HEADS, FOX_HD), lf

    outs_p = _trunk(x_prompt, gla0, lru_buf0, lru_h0, ffn_buf0, layer_c_prompt, p, w, 1, PROMPT_TILES)
    outs_s = _trunk(x_sample, state_gla, state_lru_conv, state_lru_h, state_ffn_conv, layer_c_sample, p, w, bs,
                    (ts, ts, ts, ts))
    return tuple(o for pair in zip(outs_p, outs_s) for o in pair)
```

```python
import functools

import numpy as np
import jax
import jax.numpy as jnp
from jax import lax
from jax.experimental import pallas as pl
from jax.experimental.pallas import tpu as pltpu

F32 = jnp.float32
BF16 = jnp.bfloat16
HIGHEST = lax.Precision.HIGHEST

EPS = 1e-6
GLA_HEADS = 4
GLA_DK = 64
GLA_DV = 128
GLA_GATE_NORM = 16.0
GLA_CHUNK = 64
LRU_C = 8.0
FOX_HEADS = 16
FOX_HD = 64
FOX_SCALE = FOX_HD ** -0.5
PAGE_SIZE = 128
LANES = 128
SUBLANES = 8
HIST = SUBLANES
VMEM_LIMIT = 56 * 2 ** 20
PAGES_PER_STEP = 8
GLA_SEQS = 2
GLA_CHUNKS = 4
NEG_INF = float("-inf")
LOG2E = 1.4426950408889634


def _rms(x, g):
    return x * lax.rsqrt(jnp.mean(x * x, axis=-1, keepdims=True) + EPS) * g


def _sigmoid(x):
    return 0.5 * (1.0 + jnp.tanh(0.5 * x))


def _softplus(x):
    return jnp.maximum(x, 0.0) + jnp.log1p(jnp.exp(-jnp.abs(x)))


def _log_sigmoid(x):
    return -_softplus(-x)


def _expm1(x):
    return jnp.tanh(0.5 * x) * (jnp.exp(x) + 1.0)


def _gelu(x):
    return x * (0.5 * (1.0 + jnp.tanh(0.7978845608028654 * (x + 0.044715 * (x * x * x)))))


def _dot(a, b, precision=None):
    return jnp.dot(a, b, preferred_element_type=F32, precision=precision)


def _dot_nt(a, b, precision=None):
    return lax.dot_general(a, b, (((1,), (1,)), ((), ())), preferred_element_type=F32, precision=precision)


def _dot_tn(a, b, precision=None):
    return lax.dot_general(a, b, (((0,), (0,)), ((), ())), preferred_element_type=F32, precision=precision)


def _seq_tril(m, l):
    row = lax.broadcasted_iota(jnp.int32, (m, m), 0)
    col = lax.broadcasted_iota(jnp.int32, (m, m), 1)
    same = (row & -l) == (col & -l)
    return jnp.where(same & (col <= row), 1.0, 0.0).astype(F32)


def _split3(x):
    hi = x.astype(BF16).astype(F32)
    mid = (x - hi).astype(BF16).astype(F32)
    lo = (x - hi - mid).astype(BF16).astype(F32)
    return hi, mid, lo


def _params(sem):
    return pltpu.CompilerParams(dimension_semantics=sem, vmem_limit_bytes=VMEM_LIMIT)


def _tile_spec(s, l, n):
    return pl.BlockSpec((s, l, n), lambda b, t: (b, t, 0))


def _seq_spec(s, r, n):
    return pl.BlockSpec((s, r, n), lambda b, t: (b, 0, 0))


def _const_spec(shape):
    return pl.BlockSpec(shape, lambda b, t: (0,) * len(shape))


def _norm_proj_body(x_ref, g_ref, w_ref, *o_refs):
    s, l, d = x_ref.shape
    h = _rms(x_ref[...].reshape(s * l, d), g_ref[...]).astype(BF16)
    off = 0
    for o_ref in o_refs:
        n = o_ref.shape[-1]
        o_ref[...] = _dot(h, w_ref[:, off:off + n]).reshape(s, l, n).astype(o_ref.dtype)
        off += n


def _norm_proj(x, g, w, widths, s, l):
    b, t, d = x.shape
    return pl.pallas_call(
        _norm_proj_body,
        grid=(b // s, t // l),
        in_specs=[_tile_spec(s, l, d), _const_spec((1, d)), _const_spec(w.shape)],
        out_specs=[_tile_spec(s, l, n) for n in widths],
        out_shape=[jax.ShapeDtypeStruct((b, t, n), F32) for n in widths],
        compiler_params=_params(("parallel", "parallel")),
        name="norm_proj_a",
    )(x, g, w)


def _proj_cs_body(x_ref, g_ref, w_ref, bf_ref, q_ref, k_ref, v_ref, lf_ref, c_ref):
    s, l, d = x_ref.shape
    m = s * l
    da = q_ref.shape[-1]
    nh = lf_ref.shape[-1]
    h = _rms(x_ref[...].reshape(m, d), g_ref[...]).astype(BF16)
    q_ref[...] = _dot(h, w_ref[:, 0:da]).reshape(s, l, da)
    k_ref[...] = _dot(h, w_ref[:, da:2 * da]).reshape(s, l, da)
    v_ref[...] = _dot(h, w_ref[:, 2 * da:3 * da]).reshape(s, l, da)
    lf = _log_sigmoid(_dot(h, w_ref[:, 3 * da:]) + bf_ref[...])
    c = _dot(_seq_tril(m, l), lf, HIGHEST)
    lf_ref[...] = lf[:, :nh].reshape(s, l, nh)
    c_ref[...] = c[:, :nh].reshape(s, l, nh)


def _proj_c_sample(x, g, w, bf):
    b, t, d = x.shape
    da = FOX_HEADS * FOX_HD
    wide = jax.ShapeDtypeStruct((b, t, da), F32)
    narrow = jax.ShapeDtypeStruct((b, t, FOX_HEADS), F32)
    return pl.pallas_call(
        _proj_cs_body,
        grid=(1, 1),
        in_specs=[_tile_spec(b, t, d), _const_spec((1, d)), _const_spec(w.shape), _const_spec((1, LANES))],
        out_specs=[_tile_spec(b, t, da)] * 3 + [_tile_spec(b, t, FOX_HEADS)] * 2,
        out_shape=[wide, wide, wide, narrow, narrow],
        compiler_params=_params(("parallel", "arbitrary")),
        name="norm_proj_c_sample",
    )(x, g, w, bf)


def _proj_cp_body(x_ref, g_ref, wt_ref, wk_ref, bfc_ref, eye_ref,
                  qt_ref, kt_ref, vt_ref, lft_ref, ct2_ref, kaug_ref, vaug_ref, carry_ref):
    _, l, d = x_ref.shape
    da = kt_ref.shape[1]
    h = _rms(x_ref[0], g_ref[...]).astype(BF16)
    allt = _dot_nt(wt_ref[...], h)
    qt_ref[0] = (allt[0:da] * (FOX_SCALE * LOG2E)).astype(qt_ref.dtype)
    kt_ref[0] = allt[da:2 * da]
    vt = allt[2 * da:3 * da]
    vt_ref[0] = vt
    lft = _log_sigmoid(allt[3 * da:] + bfc_ref[...])
    lft_ref[0] = lft[:FOX_HEADS]

    @pl.when(pl.program_id(1) == 0)
    def _():
        carry_ref[...] = jnp.zeros_like(carry_ref)

    row = lax.broadcasted_iota(jnp.int32, (l, l), 0)
    col = lax.broadcasted_iota(jnp.int32, (l, l), 1)
    ct = _dot(lft, jnp.where(row <= col, 1.0, 0.0).astype(F32), HIGHEST) + carry_ref[...]
    carry_ref[...] = ct[:, l - 1:l]
    ct2 = ct[:FOX_HEADS] * LOG2E
    ct2_ref[0] = ct2
    hi, mid, lo = _split3(ct2)
    ones_rows = jnp.where(lax.broadcasted_iota(jnp.int32, (FOX_HEADS, l), 0) < 3, 1.0, 0.0)
    augt = jnp.concatenate([-hi, -mid, -lo, ones_rows, jnp.zeros((LANES - 4 * FOX_HEADS, l), F32)], axis=0).astype(BF16)
    aug = _dot_tn(augt, eye_ref[...]).astype(BF16)
    kb = _dot(h, wk_ref[...]).astype(BF16)
    pair = 2 * FOX_HD
    for p in range(FOX_HEADS // 2):
        kaug_ref[0, p] = jnp.concatenate([kb[:, p * pair:(p + 1) * pair], aug], axis=1)
    vtb = vt.astype(BF16)
    ones = jnp.ones((FOX_HD, l), BF16)
    for hh in range(FOX_HEADS):
        vh = vtb[hh * FOX_HD:(hh + 1) * FOX_HD]
        vaug_ref[0, hh, 0] = jnp.concatenate([vh, ones] if hh % 2 == 0 else [ones, vh], axis=0)


def _proj_c_prompt(x, g, wt, wk, bfc, eye, l):
    b, t, d = x.shape
    da = FOX_HEADS * FOX_HD
    nk = t // l
    pair = 2 * FOX_HD
    feat_t = pl.BlockSpec((1, da, l), lambda b_, t_: (b_, 0, t_))
    head_t = pl.BlockSpec((1, FOX_HEADS, l), lambda b_, t_: (b_, 0, t_))
    return pl.pallas_call(
        _proj_cp_body,
        grid=(b, nk),
        in_specs=[_tile_spec(1, l, d), _const_spec((1, d)), _const_spec(wt.shape), _const_spec(wk.shape),
                  _const_spec((LANES, 1)), _const_spec((LANES, LANES))],
        out_specs=[feat_t, feat_t, feat_t, head_t, head_t,
                   pl.BlockSpec((1, FOX_HEADS // 2, l, pair + LANES), lambda b_, t_: (b_, 0, t_, 0)),
                   pl.BlockSpec((1, FOX_HEADS, 1, pair, l), lambda b_, t_: (b_, 0, t_, 0, 0))],
        out_shape=[jax.ShapeDtypeStruct((b, da, t), BF16),
                   jax.ShapeDtypeStruct((b, da, t), F32),
                   jax.ShapeDtypeStruct((b, da, t), F32),
                   jax.ShapeDtypeStruct((b, FOX_HEADS, t), F32),
                   jax.ShapeDtypeStruct((b, FOX_HEADS, t), F32),
                   jax.ShapeDtypeStruct((b, FOX_HEADS // 2, t, pair + LANES), BF16),
                   jax.ShapeDtypeStruct((b, FOX_HEADS, nk, pair, l), BF16)],
        scratch_shapes=[pltpu.VMEM((LANES, 1), F32)],
        compiler_params=_params(("parallel", "arbitrary")),
        name="norm_proj_c_prompt",
    )(x, g, wt, wk, bfc, eye)


def _gla_body(q_ref, k_ref, v_ref, lo_ref, gg_ref, wgk_ref, bgk_ref, ng_ref, s0_ref, o_ref, sn_ref, st_ref, *, chunk):
    s, l, dq = q_ref.shape
    nc = l // chunk
    mm = BF16 if chunk >= 16 else F32
    t = pl.program_id(1)

    @pl.when(t == 0)
    def _():
        st_ref[...] = s0_ref[...]

    tri = _seq_tril(chunk, chunk)
    lane = lax.broadcasted_iota(jnp.int32, (1, dq), 1)
    wgk = wgk_ref[...].astype(mm)
    bgk = bgk_ref[...]
    ng = ng_ref[...]

    heads = range(GLA_HEADS)
    hms = [(lane >= h * GLA_DK) & (lane < (h + 1) * GLA_DK) for h in heads]

    def do_chunks(sis, row_slices):
        jobs = [(si, rows) for si in sis for rows in row_slices]
        la = [_log_sigmoid(_dot(lo_ref[si, rows, :].astype(mm), wgk) + bgk) / GLA_GATE_NORM for si, rows in jobs]
        bcum = [_dot(tri, x, HIGHEST) for x in la]
        blast = [x[chunk - 1:chunk, :] for x in bcum]
        qm, kd, kd2, vh = [], [], [], []
        for n, (si, rows) in enumerate(jobs):
            ki = k_ref[si, rows, :]
            vi = v_ref[si, rows, :]
            qd = q_ref[si, rows, :] * (GLA_DK ** -0.5) * jnp.exp(bcum[n])
            qm.append([jnp.where(hms[h], qd, 0.0).astype(mm) for h in heads])
            kd.append((ki * jnp.exp(-bcum[n])).astype(mm))
            kd2.append((ki * jnp.exp(blast[n] - bcum[n])).astype(mm))
            vh.append([vi[:, h * GLA_DV:(h + 1) * GLA_DV].astype(mm) for h in heads])
        nj = range(len(jobs))
        att = [[_dot_nt(qm[n][h], kd[n]) for h in heads] for n in nj]
        upd = [[_dot_tn(vh[n][h], kd2[n]) for h in heads] for n in nj]
        att = [[jnp.where(tri > 0.0, att[n][h], 0.0).astype(mm) for h in heads] for n in nj]
        intra = [[_dot(att[n][h], vh[n][h]) for h in heads] for n in nj]
        st = [st_ref[si] for si in sis]
        inter = [None] * len(jobs)
        for c in range(len(row_slices)):
            ns = [u * len(row_slices) + c for u in range(len(sis))]
            for u, n in enumerate(ns):
                stm = st[u].astype(mm)
                inter[n] = [_dot_nt(qm[n][h], stm) for h in heads]
            for u, n in enumerate(ns):
                new = st[u] * jnp.exp(blast[n])
                for h in heads:
                    new = new + jnp.where(hms[h], upd[n][h], 0.0)
                st[u] = new
        for u, si in enumerate(sis):
            st_ref[si] = st[u]
        for n, (si, rows) in enumerate(jobs):
            gi = gg_ref[si, rows, :]
            for h in heads:
                gh = gi[:, h * GLA_DV:(h + 1) * GLA_DV]
                o = intra[n][h] + inter[n][h]
                o_ref[si, rows, h * GLA_DV:(h + 1) * GLA_DV] = (_rms(o, ng) * (gh * _sigmoid(gh))).astype(o_ref.dtype)

    group = min(s, GLA_SEQS)
    step = min(nc, GLA_CHUNKS)

    def do_group(gi, carry):
        sis = [gi * group + u for u in range(group)]
        if nc == step:
            do_chunks(sis, [slice(c * chunk, (c + 1) * chunk) for c in range(nc)])
        else:
            def body(ci, c2):
                do_chunks(sis, [pl.ds(pl.multiple_of((ci * step + c) * chunk, chunk), chunk) for c in range(step)])
                return c2
            lax.fori_loop(0, nc // step, body, 0)
        return carry

    if s == group:
        do_group(0, 0)
    else:
        lax.fori_loop(0, s // group, do_group, 0)

    @pl.when(t == pl.num_programs(1) - 1)
    def _():
        sn_ref[...] = st_ref[...]


def _gla(q, k, v, lo, gg, wgk, bgk, ng, s0t, s, l, chunk):
    b, t, dq = q.shape
    dv = v.shape[-1]
    return pl.pallas_call(
        functools.partial(_gla_body, chunk=chunk),
        grid=(b // s, t // l),
        in_specs=[_tile_spec(s, l, dq), _tile_spec(s, l, dq), _tile_spec(s, l, dv), _tile_spec(s, l, LANES),
                  _tile_spec(s, l, dv), _const_spec(wgk.shape), _const_spec((1, dq)), _const_spec((1, GLA_DV)),
                  _seq_spec(s, GLA_DV, dq)],
        out_specs=[_tile_spec(s, l, dv), _seq_spec(s, GLA_DV, dq)],
        out_shape=[jax.ShapeDtypeStruct((b, t, dv), F32), jax.ShapeDtypeStruct((b, GLA_DV, dq), F32)],
        scratch_shapes=[pltpu.VMEM((s, GLA_DV, dq), F32)],
        compiler_params=_params(("parallel", "arbitrary")),
        name="gla",
    )(q, k, v, lo, gg, wgk, bgk, ng, s0t)


def _lru_body(xb_ref, yb_ref, cbuf_ref, h0_ref, cw_ref, cb_ref, gw_ref, gb_ref, lam_ref,
              o_ref, ncbuf_ref, nh_ref, xbuf, hc):
    s, l, c = xb_ref.shape
    m = s * l
    nw = cw_ref.shape[0]
    t = pl.program_id(1)

    @pl.when(t == 0)
    def _():
        xbuf[:, HIST - (nw - 1):HIST, :] = cbuf_ref[...]
        hc[...] = h0_ref[...]

    @pl.when(t > 0)
    def _():
        xbuf[:, 0:HIST, :] = xbuf[:, l:l + HIST, :]

    xbuf[:, HIST:HIST + l, :] = xb_ref[...]
    xe = xbuf[...].reshape(s * (l + HIST), c)
    xc = cb_ref[...].reshape(1, 1, c)
    for j in range(nw):
        shift = nw - 1 - j
        xs = xe if shift == 0 else pltpu.roll(xe, shift, 0)
        xc = xc + xs.reshape(s, l + HIST, c)[:, HIST:, :] * cw_ref[j:j + 1, :].reshape(1, 1, c)
    xc = xc.reshape(m, c)
    gates = _dot(xc.astype(BF16), gw_ref[...]) + gb_ref[...]
    r = _sigmoid(gates[:, :c])
    i = _sigmoid(gates[:, c:])
    log_a = (-LRU_C * r) * _softplus(-lam_ref[...])
    a = jnp.exp(log_a)
    bx = jnp.sqrt(-_expm1(2.0 * log_a)) * i * xc
    a = a.reshape(m // SUBLANES, SUBLANES, c)
    bx = bx.reshape(m // SUBLANES, SUBLANES, c)
    pos = lax.broadcasted_iota(jnp.int32, a.shape, 1)
    d = 1
    while d < SUBLANES:
        valid = pos >= d
        a_prev = jnp.where(valid, pltpu.roll(a, d, 1), 1.0)
        b_prev = jnp.where(valid, pltpu.roll(bx, d, 1), 0.0)
        bx = a * b_prev + bx
        a = a * a_prev
        d *= 2
    a = a.reshape(s, l, c)
    bx = bx.reshape(s, l, c)
    h_prev = hc[...]
    groups = []
    for gi in range(l // SUBLANES):
        rows = slice(gi * SUBLANES, (gi + 1) * SUBLANES)
        hg = bx[:, rows, :] + a[:, rows, :] * h_prev
        groups.append(hg)
        h_prev = hg[:, SUBLANES - 1:SUBLANES, :]
    hs = jnp.concatenate(groups, axis=1).reshape(m, c)
    o_ref[...] = (hs * _gelu(yb_ref[...].reshape(m, c))).reshape(s, l, c)
    hc[...] = h_prev

    @pl.when(t == pl.num_programs(1) - 1)
    def _():
        ncbuf_ref[...] = xbuf[:, l + HIST - (nw - 1):l + HIST, :]
        nh_ref[...] = hc[...]


def _lru(xb, yb, cbuf, h0, cw, cb, gw, gb, lam, s, l):
    b, t, c = xb.shape
    nw = cw.shape[0]
    return pl.pallas_call(
        _lru_body,
        grid=(b // s, t // l),
        in_specs=[_tile_spec(s, l, c), _tile_spec(s, l, c), _seq_spec(s, nw - 1, c), _seq_spec(s, 1, c),
                  _const_spec(cw.shape), _const_spec((1, c)), _const_spec(gw.shape), _const_spec((1, 2 * c)),
                  _const_spec((1, c))],
        out_specs=[_tile_spec(s, l, c), _seq_spec(s, nw - 1, c), _seq_spec(s, 1, c)],
        out_shape=[jax.ShapeDtypeStruct((b, t, c), F32), jax.ShapeDtypeStruct((b, nw - 1, c), F32),
                   jax.ShapeDtypeStruct((b, 1, c), F32)],
        scratch_shapes=[pltpu.VMEM((s, l + HIST, c), F32), pltpu.VMEM((s, 1, c), F32)],
        compiler_params=_params(("parallel", "arbitrary")),
        name="rg_lru",
    )(xb, yb, cbuf, h0, cw, cb, gw, gb, lam)


def _out_proj_body(*refs, n_in):
    a_refs, w_refs = refs[:n_in], refs[n_in:2 * n_in]
    x_ref, g_ref, y_ref = refs[2 * n_in:]
    s, l, d = x_ref.shape
    m = s * l
    acc = None
    for a_ref, w_ref in zip(a_refs, w_refs):
        part = _dot(a_ref[...].reshape(m, a_ref.shape[-1]).astype(BF16), w_ref[...])
        acc = part if acc is None else acc + part
    y_ref[...] = (x_ref[...].reshape(m, d) + _rms(acc, g_ref[...])).reshape(s, l, d)


def _out_proj(acts, ws, x, g, s, l):
    b, t, d = x.shape
    n_in = len(acts)
    return pl.pallas_call(
        functools.partial(_out_proj_body, n_in=n_in),
        grid=(b // s, t // l),
        in_specs=[_tile_spec(s, l, a.shape[-1]) for a in acts] + [_const_spec(w.shape) for w in ws]
                 + [_tile_spec(s, l, d), _const_spec((1, d))],
        out_specs=_tile_spec(s, l, d),
        out_shape=jax.ShapeDtypeStruct((b, t, d), F32),
        compiler_params=_params(("parallel", "parallel")),
        name="out_proj",
    )(*acts, *ws, x, g)


def _ffn_body(x_ref, buf_ref, g2_ref, g3_ref, wup_ref, cw_ref, cb_ref, wdn_ref, y_ref, nbuf_ref, gbuf):
    s, l, d = x_ref.shape
    m = s * l
    nw, f = cw_ref.shape
    t = pl.program_id(1)
    x = x_ref[...].reshape(m, d)
    h = _rms(x, g2_ref[...]).astype(BF16)

    @pl.when(t == 0)
    def _():
        gbuf[:, HIST - (nw - 1):HIST, :] = buf_ref[...]

    @pl.when(t > 0)
    def _():
        gbuf[:, 0:HIST, :] = gbuf[:, l:l + HIST, :]

    gbuf[:, HIST:HIST + l, :] = _dot(h, wup_ref[:, :f]).reshape(s, l, f)
    u = _dot(h, wup_ref[:, f:])
    gc = cb_ref[...].reshape(1, 1, f)
    for j in range(nw):
        off = HIST - (nw - 1) + j
        gc = gc + gbuf[:, off:off + l, :] * cw_ref[j:j + 1, :].reshape(1, 1, f)
    act = (_gelu(gc).reshape(m, f) * u).astype(BF16)
    y = _dot(act, wdn_ref[...])
    y_ref[...] = (x + _rms(y, g3_ref[...])).reshape(s, l, d)

    @pl.when(t == pl.num_programs(1) - 1)
    def _():
        nbuf_ref[...] = gbuf[:, l + HIST - (nw - 1):l + HIST, :]


def _ffn(x, buf, g2, g3, wup, cw, cb, wdn, s, l):
    b, t, d = x.shape
    nw, f = cw.shape
    return pl.pallas_call(
        _ffn_body,
        grid=(b // s, t // l),
        in_specs=[_tile_spec(s, l, d), _seq_spec(s, nw - 1, f), _const_spec((1, d)), _const_spec((1, d)),
                  pl.BlockSpec(wup.shape, lambda b_, t_: (0, 0), pipeline_mode=pl.Buffered(1)),
                  _const_spec(cw.shape), _const_spec((1, f)),
                  pl.BlockSpec(wdn.shape, lambda b_, t_: (0, 0), pipeline_mode=pl.Buffered(1))],
        out_specs=[_tile_spec(s, l, d), _seq_spec(s, nw - 1, f)],
        out_shape=[jax.ShapeDtypeStruct((b, t, d), F32), jax.ShapeDtypeStruct((b, nw - 1, f), F32)],
        scratch_shapes=[pltpu.VMEM((s, l + HIST, f), F32)],
        compiler_params=_params(("parallel", "arbitrary")),
        name="conv_ffn",
    )(x, buf, g2, g3, wup, cw, cb, wdn)


def _fox_prompt_body(qt_ref, ct_ref, kaug_ref, vaug_ref, o_ref, m_ref, acc_ref, sa_ref, sb_ref, *, tq):
    hp = pl.program_id(1)
    iq = pl.program_id(2)
    qt = qt_ref[0]
    feat = lax.broadcasted_iota(jnp.int32, (LANES, 1), 0)
    qas = []
    for j in range(2):
        head = 2 * hp + j
        hi, mid, lo = _split3(ct_ref[0, pl.ds(head, 1), :])
        pick = jnp.where(((feat & (FOX_HEADS - 1)) == head) & (feat < 3 * FOX_HEADS), 1.0, 0.0)
        augt = jnp.where(feat == 3 * FOX_HEADS, hi,
                         jnp.where(feat == 3 * FOX_HEADS + 1, mid, jnp.where(feat == 3 * FOX_HEADS + 2, lo, pick)))
        own = (feat < FOX_HD) if j == 0 else (feat >= FOX_HD)
        qas.append(jnp.concatenate([jnp.where(own, qt, jnp.zeros_like(qt)), augt.astype(BF16)], axis=0))
    qa = jnp.concatenate(qas, axis=1)
    m_ref[...] = jnp.full_like(m_ref, NEG_INF)
    acc_ref[...] = jnp.zeros_like(acc_ref)
    key = lax.broadcasted_iota(jnp.int32, (tq, tq), 0)
    qry = lax.broadcasted_iota(jnp.int32, (tq, tq), 1)

    def logits(ik, s_ref):
        kt = kaug_ref[0, 0, pl.ds(pl.multiple_of(ik * tq, tq), tq), :]
        s_ref[...] = _dot(kt, qa)

    def update(ik, s_ref, masked):
        for j in range(2):
            st = s_ref[:, j * tq:(j + 1) * tq]
            if masked:
                st = jnp.where(key <= qry, st, NEG_INF)
            m_old = m_ref[j]
            m_new = jnp.maximum(m_old, jnp.max(st, axis=0, keepdims=True))
            pt = jnp.exp2(st - m_new).astype(BF16)
            m_ref[j] = m_new
            acc_ref[j] = acc_ref[j] * jnp.exp2(m_old - m_new) + _dot(vaug_ref[0, j, ik], pt)

    logits(0, sa_ref)

    def pair_body(i, carry):
        logits(2 * i + 1, sb_ref)
        update(2 * i, sa_ref, False)
        logits(2 * i + 2, sa_ref)
        update(2 * i + 1, sb_ref, False)
        return carry

    lax.fori_loop(0, iq >> 1, pair_body, 0)

    @pl.when((iq & 1) == 0)
    def _():
        update(iq, sa_ref, True)

    @pl.when((iq & 1) == 1)
    def _():
        logits(iq, sb_ref)
        update(iq - 1, sa_ref, False)
        update(iq, sb_ref, True)

    a0 = acc_ref[0]
    a1 = acc_ref[1]
    ot = jnp.concatenate([a0[:FOX_HD] / a0[FOX_HD:], a1[FOX_HD:] / a1[:FOX_HD]], axis=0)
    o_ref[0] = ot.T.astype(o_ref.dtype)


def _fox_prompt(qt, ct2, kaug, vaug, tq):
    b, da, t = qt.shape
    nq = t // tq
    assert vaug.shape[2] == nq and vaug.shape[-1] == tq
    return pl.pallas_call(
        functools.partial(_fox_prompt_body, tq=tq),
        grid=(b, FOX_HEADS // 2, nq),
        in_specs=[pl.BlockSpec((1, LANES, tq), lambda b_, h_, i_: (b_, h_, i_)),
                  pl.BlockSpec((1, FOX_HEADS, tq), lambda b_, h_, i_: (b_, 0, i_)),
                  pl.BlockSpec((1, 1) + kaug.shape[2:], lambda b_, h_, i_: (b_, h_, 0, 0)),
                  pl.BlockSpec((1, 2) + vaug.shape[2:], lambda b_, h_, i_: (b_, h_, 0, 0, 0))],
        out_specs=pl.BlockSpec((1, tq, LANES), lambda b_, h_, i_: (b_, i_, h_)),
        out_shape=jax.ShapeDtypeStruct((b, t, da), BF16),
        scratch_shapes=[pltpu.VMEM((2, 1, tq), F32), pltpu.VMEM((2, LANES, tq), F32),
                        pltpu.VMEM((tq, 2 * tq), F32), pltpu.VMEM((tq, 2 * tq), F32)],
        compiler_params=_params(("parallel", "parallel", "arbitrary")),
        name="fox_prompt",
    )(qt, ct2, kaug, vaug)


def _fox_sample_body(pt_ref, q_ref, kn_ref, vn_ref, cnt_ref, cnr_ref, *rest, npp):
    kp = rest[0:npp]
    vp = rest[npp:2 * npp]
    lp = rest[2 * npp:3 * npp]
    o_ref, m_ref, l_ref, acc_ref, carry_ref, kb_ref, vb_ref = rest[3 * npp:]
    del pt_ref
    j = pl.program_id(1)
    t, da = q_ref.shape[1], q_ref.shape[2]
    rows = FOX_HEADS * t
    row_c = lax.broadcasted_iota(jnp.int32, (rows, da), 0)
    col_c = lax.broadcasted_iota(jnp.int32, (rows, da), 1)
    own = (col_c & -FOX_HD) == (row_c & -t) * (FOX_HD // t)
    q8 = q_ref[0] * FOX_SCALE
    qbd = jnp.where(own, jnp.concatenate([q8] * FOX_HEADS, axis=0), 0.0).astype(BF16)
    cn_rows = cnr_ref[0]

    def head_rows(x):
        return jnp.concatenate([jnp.broadcast_to(x[h:h + 1, :], (t, x.shape[1])) for h in range(FOX_HEADS)], axis=0)

    @pl.when(j == 0)
    def _():
        pad = jnp.zeros((PAGE_SIZE - t, da), F32)
        kn = jnp.concatenate([kn_ref[0], pad], axis=0).astype(BF16)
        vn = jnp.concatenate([vn_ref[0], pad], axis=0).astype(BF16)
        row_l = lax.broadcasted_iota(jnp.int32, (rows, PAGE_SIZE), 0)
        col_l = lax.broadcasted_iota(jnp.int32, (rows, PAGE_SIZE), 1)
        sc = _dot_nt(qbd, kn) + cn_rows - head_rows(cnt_ref[0])
        sc = jnp.where(col_l <= (row_l & (t - 1)), sc, NEG_INF)
        m0 = jnp.max(sc, axis=-1, keepdims=True)
        p = jnp.exp(sc - m0)
        m_ref[...] = m0
        l_ref[...] = jnp.sum(p, axis=-1, keepdims=True)
        acc_ref[...] = _dot(p.astype(BF16), vn)
        carry_ref[...] = jnp.zeros_like(carry_ref)

    prow = lax.broadcasted_iota(jnp.int32, (PAGE_SIZE, PAGE_SIZE), 0)
    pcol = lax.broadcasted_iota(jnp.int32, (PAGE_SIZE, PAGE_SIZE), 1)
    later = jnp.where(prow > pcol, 1.0, 0.0).astype(F32)
    for i in range(npp):
        kb_ref[:, i * PAGE_SIZE:(i + 1) * PAGE_SIZE] = kp[i][...].astype(BF16)
        vb_ref[:, i * PAGE_SIZE:(i + 1) * PAGE_SIZE] = vp[i][...].astype(BF16)
    lf_all = jnp.concatenate([lp[i][...] for i in range(npp)], axis=0)
    inside = _dot(lf_all, later, HIGHEST)
    biases = []
    carry = carry_ref[...]
    for i in range(npp):
        rows_i = slice(i * FOX_HEADS, (i + 1) * FOX_HEADS)
        biases.append(head_rows(inside[rows_i] + carry))
        carry = carry + (inside[rows_i, 0:1] + lf_all[rows_i, 0:1])
    carry_ref[...] = carry
    sc = _dot(qbd, kb_ref[...]) + cn_rows + jnp.concatenate(biases, axis=1)
    m_old = m_ref[...]
    m_new = jnp.maximum(m_old, jnp.max(sc, axis=-1, keepdims=True))
    alpha = jnp.exp(m_old - m_new)
    p = jnp.exp(sc - m_new)
    l_new = alpha * l_ref[...] + jnp.sum(p, axis=-1, keepdims=True)
    acc = acc_ref[...] * alpha + _dot_nt(p.astype(BF16), vb_ref[...])
    m_ref[...] = m_new
    l_ref[...] = l_new
    acc_ref[...] = acc

    @pl.when(j == pl.num_programs(1) - 1)
    def _():
        full = jnp.where(own, acc / l_new, 0.0)
        out = full[0:t, :]
        for h in range(1, FOX_HEADS):
            out = out + full[h * t:(h + 1) * t, :]
        o_ref[0] = out


def _fox_sample(q, k, v, c, cache_k, cache_v, cache_logf, page_table, layer):
    b, t, da = q.shape
    n_pages = page_table.shape[1]
    npp = PAGES_PER_STEP
    assert n_pages % npp == 0 and PAGE_SIZE % t == 0 and t % SUBLANES == 0
    n_pool, n_layers = cache_k.shape[0], cache_k.shape[1]
    ckt = cache_k.transpose(0, 1, 3, 4, 2).reshape(n_pool, n_layers, da, PAGE_SIZE)
    cvt = cache_v.transpose(0, 1, 3, 4, 2).reshape(n_pool, n_layers, da, PAGE_SIZE)
    clt = cache_logf.transpose(0, 1, 3, 2)
    ct = c.transpose(0, 2, 1)
    cn_t = jnp.pad(ct, ((0, 0), (0, 0), (0, PAGE_SIZE - t)))
    cn_rows = ct.reshape(b, FOX_HEADS * t, 1)

    def page_spec(i, height):
        return pl.BlockSpec((None, None, height, PAGE_SIZE),
                            lambda b_, j_, pt: (pt[b_, n_pages - 1 - (j_ * npp + i)], layer, 0, 0))

    tok = pl.BlockSpec((1, t, da), lambda b_, j_, pt: (b_, 0, 0))
    rows = FOX_HEADS * t
    grid_spec = pltpu.PrefetchScalarGridSpec(
        num_scalar_prefetch=1,
        grid=(b, n_pages // npp),
        in_specs=[tok, tok, tok,
                  pl.BlockSpec((1, FOX_HEADS, PAGE_SIZE), lambda b_, j_, pt: (b_, 0, 0)),
                  pl.BlockSpec((1, rows, 1), lambda b_, j_, pt: (b_, 0, 0))]
                 + [page_spec(i, da) for i in range(npp)] * 2
                 + [page_spec(i, FOX_HEADS) for i in range(npp)],
        out_specs=tok,
        scratch_shapes=[pltpu.VMEM((rows, 1), F32), pltpu.VMEM((rows, 1), F32), pltpu.VMEM((rows, da), F32),
                        pltpu.VMEM((FOX_HEADS, 1), F32),
                        pltpu.VMEM((da, npp * PAGE_SIZE), BF16), pltpu.VMEM((da, npp * PAGE_SIZE), BF16)],
    )
    return pl.pallas_call(
        functools.partial(_fox_sample_body, npp=npp),
        grid_spec=grid_spec,
        out_shape=jax.ShapeDtypeStruct((b, t, da), F32),
        compiler_params=_params(("parallel", "arbitrary")),
        name="fox_sample",
    )(page_table, q, k, v, cn_t, cn_rows, *([ckt] * npp), *([cvt] * npp), *([clt] * npp))


def _pow2_tile(t, want):
    l = min(t, want)
    assert t % l == 0 and l & (l - 1) == 0
    return l


def _prep_weights(p):
    w = {}
    n_a = p['w_in_a'].shape[0]
    wa = p['w_in_a']
    d = wa.shape[1]
    q_end = 2 * GLA_HEADS * GLA_DK + GLA_HEADS * GLA_DV
    rank = p['w_gk2'].shape[1]
    w['w_in_a'] = jnp.concatenate(
        [wa[:, :, :q_end], wa[:, :, q_end + rank:], wa[:, :, q_end:q_end + rank],
         jnp.zeros((n_a, d, LANES - rank), wa.dtype)], axis=-1).astype(BF16)
    w['w_gk2'] = jnp.pad(p['w_gk2'], ((0, 0), (0, LANES - rank), (0, 0)))
    gw = p['lru_gate_w']
    nb, bw = gw.shape[2], gw.shape[3]
    eye = jnp.eye(nb, dtype=gw.dtype)
    dense = jnp.einsum('agncd,nm->agncmd', gw, eye).reshape(n_a, 2, nb * bw, nb * bw)
    w['lru_gate_w'] = jnp.concatenate([dense[:, 0], dense[:, 1]], axis=-1).astype(BF16)
    w['lru_gate_b'] = p['lru_gate_b'].reshape(n_a, 1, -1)
    w['w_out_a'] = p['w_out_a'].astype(BF16)
    wc = p['w_in_c']
    n_c = wc.shape[0]
    da = FOX_HEADS * FOX_HD
    w['w_in_c'] = jnp.concatenate([wc, jnp.zeros((n_c, d, LANES - FOX_HEADS), wc.dtype)], axis=-1).astype(BF16)
    w['b_f'] = jnp.pad(p['b_f'], ((0, 0), (0, LANES - FOX_HEADS))).reshape(n_c, 1, LANES)
    w['w_qkvf_t'] = jnp.swapaxes(w['w_in_c'], 1, 2)
    w['w_k_c'] = w['w_in_c'][:, :, da:2 * da]
    w['b_f_col'] = w['b_f'].reshape(n_c, LANES, 1)
    w['eye'] = jnp.eye(LANES, dtype=BF16)
    w['w_out_c'] = p['w_out_c'].astype(BF16)
    w['ffn_w_up'] = p['ffn_w_up'].astype(BF16)
    w['ffn_w_down'] = p['ffn_w_down'].astype(BF16)
    return w


def _trunk(x, gla_s, lru_buf, lru_h, ffn_buf, layer_c, p, w, s, tiles):
    b, t, d = x.shape
    depth = p['norm_g'].shape[0]
    l_proj, l_gla, l_lru, l_ffn = (_pow2_tile(t, n) for n in tiles)
    chunk = np.gcd(t, GLA_CHUNK).item()
    a_widths = [GLA_HEADS * GLA_DK] * 2 + [GLA_HEADS * GLA_DV] * 2 + [lru_buf.shape[-1]] * 2 + [LANES]
    w_split = GLA_HEADS * GLA_DV
    gla_l, buf_l, h_l, k_l, v_l, lf_l, ffn_l = [], [], [], [], [], [], []
    ia = ic = 0
    for layer in range(depth):
        ng = p['norm_g'][layer].reshape(4, 1, d)
        if layer % 2 == 0:
            q, k, v, gg, xb, yb, lo = _norm_proj(x, ng[0], w['w_in_a'][ia], a_widths, s, l_proj)
            s0t = gla_s[:, ia].reshape(b, GLA_HEADS * GLA_DK, GLA_DV).transpose(0, 2, 1)
            o, snt = _gla(q, k, v, lo, gg, w['w_gk2'][ia], p['b_gk2'][ia].reshape(1, -1),
                          p['gla_norm_g'][ia].reshape(1, -1), s0t, s if s > 1 else min(b, GLA_SEQS), l_gla, chunk)
            lru_o, nbuf, nh = _lru(xb, yb, lru_buf[:, ia], lru_h[:, ia][:, None, :], p['lru_conv_w'][ia],
                                   p['lru_conv_b'][ia].reshape(1, -1), w['lru_gate_w'][ia], w['lru_gate_b'][ia],
                                   p['lru_lambda'][ia].reshape(1, -1), s, l_lru)
            x = _out_proj([o, lru_o], [w['w_out_a'][ia][:w_split], w['w_out_a'][ia][w_split:]], x, ng[1], s, l_proj)
            gla_l.append(snt.transpose(0, 2, 1).reshape(b, GLA_HEADS, GLA_DK, GLA_DV))
            buf_l.append(nbuf)
            h_l.append(nh[:, 0, :])
            ia += 1
        else:
            o, k_new, v_new, lf_new = layer_c(x, ng[0], ic)
            x = _out_proj([o], [w['w_out_c'][ic]], x, ng[1], s, l_proj)
            k_l.append(k_new)
            v_l.append(v_new)
            lf_l.append(lf_new)
            ic += 1
        x, fb = _ffn(x, ffn_buf[:, layer], ng[2], ng[3], w['ffn_w_up'][layer], p['ffn_conv_w'][layer],
                     p['ffn_conv_b'][layer].reshape(1, -1), w['ffn_w_down'][layer], s, l_ffn)
        ffn_l.append(fb)
    return (x, jnp.stack(gla_l, axis=1), jnp.stack(buf_l, axis=1), jnp.stack(h_l, axis=1), jnp.stack(k_l, axis=1),
            jnp.stack(v_l, axis=1), jnp.stack(lf_l, axis=1), jnp.stack(ffn_l, axis=1))


PROMPT_TILES = (512, 512, 256, 512)
FOX_TQ = 512


def kernel(x_prompt, x_sample, state_gla, state_lru_conv, state_lru_h, cache_k, cache_v, cache_logf, state_ffn_conv, page_table, norm_g, w_in_a, w_gk2, b_gk2, gla_norm_g, lru_conv_w, lru_conv_b, lru_gate_w, lru_gate_b, lru_lambda, w_out_a, w_in_c, b_f, w_out_c, ffn_w_up, ffn_conv_w, ffn_conv_b, ffn_w_down):
    p = {'norm_g': norm_g, 'w_in_a': w_in_a, 'w_gk2': w_gk2, 'b_gk2': b_gk2, 'gla_norm_g': gla_norm_g,
         'lru_conv_w': lru_conv_w, 'lru_conv_b': lru_conv_b, 'lru_gate_w': lru_gate_w, 'lru_gate_b': lru_gate_b,
         'lru_lambda': lru_lambda, 'w_out_a': w_out_a, 'w_in_c': w_in_c, 'b_f': b_f, 'w_out_c': w_out_c,
         'ffn_w_up': ffn_w_up, 'ffn_conv_w': ffn_conv_w, 'ffn_conv_b': ffn_conv_b, 'ffn_w_down': ffn_w_down}
    w = _prep_weights(p)
    bp = x_prompt.shape[0]
    bs, ts, _ = x_sample.shape
    dt = x_prompt.dtype
    gla0 = jnp.zeros((bp,) + state_gla.shape[1:], dt)
    lru_buf0 = jnp.zeros((bp,) + state_lru_conv.shape[1:], dt)
    lru_h0 = jnp.zeros((bp,) + state_lru_h.shape[1:], dt)
    ffn_buf0 = jnp.zeros((bp,) + state_ffn_conv.shape[1:], dt)

    def layer_c_prompt(x, g, ic):
        b, t, _ = x.shape
        l = _pow2_tile(t, FOX_TQ)
        qt, kt, vt, lft, ct2, kaug, vaug = _proj_c_prompt(x, g, w['w_qkvf_t'][ic], w['w_k_c'][ic], w['b_f_col'][ic],
                                                          w['eye'], l)
        o = _fox_prompt(qt, ct2, kaug, vaug, l)
        k_new = kt.reshape(b, FOX_HEADS, FOX_HD, t).transpose(0, 3, 1, 2)
        v_new = vt.reshape(b, FOX_HEADS, FOX_HD, t).transpose(0, 3, 1, 2)
        return o, k_new, v_new, lft.transpose(0, 2, 1)

    def layer_c_sample(x, g, ic):
        b, t, _ = x.shape
        q, k, v, lf, c = _proj_c_sample(x, g, w['w_in_c'][ic], w['b_f'][ic])
        o = _fox_sample(q, k, v, c, cache_k, cache_v, cache_logf, page_table, ic)
        return o, k.reshape(b, t, FOX_HEADS, FOX_HD), v.reshape(b, t, FOX_HEADS, FOX_HD), lf

    outs_p = _trunk(x_prompt, gla0, lru_buf0, lru_h0, ffn_buf0, layer_c_prompt, p, w, 1, PROMPT_TILES)
    outs_s = _trunk(x_sample, state_gla, state_lru_conv, state_lru_h, state_ffn_conv, layer_c_sample, p, w, bs,
                    (ts, ts, ts, ts))
    return tuple(o for pair in zip(outs_p, outs_s) for o in pair)
```

```python
import functools

import numpy as np
import jax
import jax.numpy as jnp
from jax import lax
from jax.experimental import pallas as pl
from jax.experimental.pallas import tpu as pltpu

F32 = jnp.float32
BF16 = jnp.bfloat16
HIGHEST = lax.Precision.HIGHEST

EPS = 1e-6
GLA_HEADS = 4
GLA_DK = 64
GLA_DV = 128
GLA_GATE_NORM = 16.0
GLA_CHUNK = 64
LRU_C = 8.0
FOX_HEADS = 16
FOX_HD = 64
FOX_SCALE = FOX_HD ** -0.5
PAGE_SIZE = 128
LANES = 128
SUBLANES = 8
HIST = SUBLANES
VMEM_LIMIT = 56 * 2 ** 20
PAGES_PER_STEP = 8
GLA_SEQS = 2
GLA_CHUNKS = 4
NEG_INF = float("-inf")
LOG2E = 1.4426950408889634


def _rms(x, g):
    return x * lax.rsqrt(jnp.mean(x * x, axis=-1, keepdims=True) + EPS) * g


def _sigmoid(x):
    return 0.5 * (1.0 + jnp.tanh(0.5 * x))


def _softplus(x):
    return jnp.maximum(x, 0.0) + jnp.log1p(jnp.exp(-jnp.abs(x)))


def _log_sigmoid(x):
    return -_softplus(-x)


def _expm1(x):
    return jnp.tanh(0.5 * x) * (jnp.exp(x) + 1.0)


def _gelu(x):
    return x * (0.5 * (1.0 + jnp.tanh(0.7978845608028654 * (x + 0.044715 * (x * x * x)))))


def _dot(a, b, precision=None):
    return jnp.dot(a, b, preferred_element_type=F32, precision=precision)


def _dot_nt(a, b, precision=None):
    return lax.dot_general(a, b, (((1,), (1,)), ((), ())), preferred_element_type=F32, precision=precision)


def _dot_tn(a, b, precision=None):
    return lax.dot_general(a, b, (((0,), (0,)), ((), ())), preferred_element_type=F32, precision=precision)


def _seq_tril(m, l):
    row = lax.broadcasted_iota(jnp.int32, (m, m), 0)
    col = lax.broadcasted_iota(jnp.int32, (m, m), 1)
    same = (row & -l) == (col & -l)
    return jnp.where(same & (col <= row), 1.0, 0.0).astype(F32)


def _split3(x):
    hi = x.astype(BF16).astype(F32)
    mid = (x - hi).astype(BF16).astype(F32)
    lo = (x - hi - mid).astype(BF16).astype(F32)
    return hi, mid, lo


def _act_dtype(rows):
    return BF16 if rows % (2 * SUBLANES) == 0 else F32


def _params(sem):
    return pltpu.CompilerParams(dimension_semantics=sem, vmem_limit_bytes=VMEM_LIMIT)


def _tile_spec(s, l, n):
    return pl.BlockSpec((s, l, n), lambda b, t: (b, t, 0))


def _seq_spec(s, r, n):
    return pl.BlockSpec((s, r, n), lambda b, t: (b, 0, 0))


def _const_spec(shape):
    return pl.BlockSpec(shape, lambda b, t: (0,) * len(shape))


def _norm_proj_body(x_ref, g_ref, w_ref, *o_refs):
    s, l, d = x_ref.shape
    h = _rms(x_ref[...].reshape(s * l, d), g_ref[...]).astype(BF16)
    off = 0
    for o_ref in o_refs:
        n = o_ref.shape[-1]
        o_ref[...] = _dot(h, w_ref[:, off:off + n]).reshape(s, l, n).astype(o_ref.dtype)
        off += n


def _norm_proj(x, g, w, widths, s, l):
    b, t, d = x.shape
    return pl.pallas_call(
        _norm_proj_body,
        grid=(b // s, t // l),
        in_specs=[_tile_spec(s, l, d), _const_spec((1, d)), _const_spec(w.shape)],
        out_specs=[_tile_spec(s, l, n) for n in widths],
        out_shape=[jax.ShapeDtypeStruct((b, t, n), F32) for n in widths],
        compiler_params=_params(("parallel", "parallel")),
        name="norm_proj_a",
    )(x, g, w)


def _proj_cs_body(x_ref, g_ref, w_ref, bf_ref, q_ref, k_ref, v_ref, lf_ref, c_ref):
    s, l, d = x_ref.shape
    m = s * l
    da = q_ref.shape[-1]
    nh = lf_ref.shape[-1]
    h = _rms(x_ref[...].reshape(m, d), g_ref[...]).astype(BF16)
    q_ref[...] = _dot(h, w_ref[:, 0:da]).reshape(s, l, da)
    k_ref[...] = _dot(h, w_ref[:, da:2 * da]).reshape(s, l, da)
    v_ref[...] = _dot(h, w_ref[:, 2 * da:3 * da]).reshape(s, l, da)
    lf = _log_sigmoid(_dot(h, w_ref[:, 3 * da:]) + bf_ref[...])
    c = _dot(_seq_tril(m, l), lf, HIGHEST)
    lf_ref[...] = lf[:, :nh].reshape(s, l, nh)
    c_ref[...] = c[:, :nh].reshape(s, l, nh)


def _proj_c_sample(x, g, w, bf):
    b, t, d = x.shape
    da = FOX_HEADS * FOX_HD
    wide = jax.ShapeDtypeStruct((b, t, da), F32)
    narrow = jax.ShapeDtypeStruct((b, t, FOX_HEADS), F32)
    return pl.pallas_call(
        _proj_cs_body,
        grid=(1, 1),
        in_specs=[_tile_spec(b, t, d), _const_spec((1, d)), _const_spec(w.shape), _const_spec((1, LANES))],
        out_specs=[_tile_spec(b, t, da)] * 3 + [_tile_spec(b, t, FOX_HEADS)] * 2,
        out_shape=[wide, wide, wide, narrow, narrow],
        compiler_params=_params(("parallel", "arbitrary")),
        name="norm_proj_c_sample",
    )(x, g, w, bf)


def _proj_cp_body(x_ref, g_ref, wt_ref, bfc_ref, eye_ref,
                  qt_ref, kt_ref, vt_ref, lft_ref, ct2_ref, kaug_ref, vaug_ref, carry_ref):
    _, l, d = x_ref.shape
    da = kt_ref.shape[1]
    h = _rms(x_ref[0], g_ref[...]).astype(BF16)
    allt = _dot_nt(wt_ref[...], h)
    qt_ref[0] = (allt[0:da] * (FOX_SCALE * LOG2E)).astype(qt_ref.dtype)
    kt = allt[da:2 * da]
    kt_ref[0] = kt
    vt = allt[2 * da:3 * da]
    vt_ref[0] = vt
    lft = _log_sigmoid(allt[3 * da:] + bfc_ref[...])
    lft_ref[0] = lft[:FOX_HEADS]

    @pl.when(pl.program_id(1) == 0)
    def _():
        carry_ref[...] = jnp.zeros_like(carry_ref)

    row = lax.broadcasted_iota(jnp.int32, (l, l), 0)
    col = lax.broadcasted_iota(jnp.int32, (l, l), 1)
    ct = _dot(lft, jnp.where(row <= col, 1.0, 0.0).astype(F32), HIGHEST) + carry_ref[...]
    carry_ref[...] = ct[:, l - 1:l]
    ct2 = ct[:FOX_HEADS] * LOG2E
    ct2_ref[0] = ct2
    hi, mid, lo = _split3(ct2)
    ones_rows = jnp.where(lax.broadcasted_iota(jnp.int32, (FOX_HEADS, l), 0) < 3, 1.0, 0.0)
    augt = jnp.concatenate([-hi, -mid, -lo, ones_rows, jnp.zeros((LANES - 4 * FOX_HEADS, l), F32)], axis=0).astype(BF16)
    aug = _dot_tn(augt, eye_ref[...]).astype(BF16)
    pair = 2 * FOX_HD
    for p in range(FOX_HEADS // 2):
        kaug_ref[0, p] = jnp.concatenate([kt[p * pair:(p + 1) * pair].T.astype(BF16), aug], axis=1)
    vtb = vt.astype(BF16)
    ones = jnp.ones((FOX_HD, l), BF16)
    for hh in range(FOX_HEADS):
        vh = vtb[hh * FOX_HD:(hh + 1) * FOX_HD]
        vaug_ref[0, hh, 0] = jnp.concatenate([vh, ones] if hh % 2 == 0 else [ones, vh], axis=0)


def _proj_c_prompt(x, g, wt, bfc, eye, l):
    b, t, d = x.shape
    da = FOX_HEADS * FOX_HD
    nk = t // l
    pair = 2 * FOX_HD
    feat_t = pl.BlockSpec((1, da, l), lambda b_, t_: (b_, 0, t_))
    head_t = pl.BlockSpec((1, FOX_HEADS, l), lambda b_, t_: (b_, 0, t_))
    return pl.pallas_call(
        _proj_cp_body,
        grid=(b, nk),
        in_specs=[_tile_spec(1, l, d), _const_spec((1, d)), _const_spec(wt.shape),
                  _const_spec((LANES, 1)), _const_spec((LANES, LANES))],
        out_specs=[feat_t, feat_t, feat_t, head_t, head_t,
                   pl.BlockSpec((1, FOX_HEADS // 2, l, pair + LANES), lambda b_, t_: (b_, 0, t_, 0)),
                   pl.BlockSpec((1, FOX_HEADS, 1, pair, l), lambda b_, t_: (b_, 0, t_, 0, 0))],
        out_shape=[jax.ShapeDtypeStruct((b, da, t), BF16),
                   jax.ShapeDtypeStruct((b, da, t), F32),
                   jax.ShapeDtypeStruct((b, da, t), F32),
                   jax.ShapeDtypeStruct((b, FOX_HEADS, t), F32),
                   jax.ShapeDtypeStruct((b, FOX_HEADS, t), F32),
                   jax.ShapeDtypeStruct((b, FOX_HEADS // 2, t, pair + LANES), BF16),
                   jax.ShapeDtypeStruct((b, FOX_HEADS, nk, pair, l), BF16)],
        scratch_shapes=[pltpu.VMEM((LANES, 1), F32)],
        compiler_params=_params(("parallel", "arbitrary")),
        name="norm_proj_c_prompt",
    )(x, g, wt, bfc, eye)


def _gla_body(q_ref, k_ref, v_ref, lo_ref, gg_ref, wgk_ref, bgk_ref, ng_ref, s0_ref, o_ref, sn_ref, st_ref, *, chunk):
    s, l, dq = q_ref.shape
    nc = l // chunk
    mm = BF16 if chunk >= 16 else F32
    t = pl.program_id(1)

    @pl.when(t == 0)
    def _():
        st_ref[...] = s0_ref[...]

    tri = _seq_tril(chunk, chunk)
    lane = lax.broadcasted_iota(jnp.int32, (1, dq), 1)
    wgk = wgk_ref[...].astype(mm)
    bgk = bgk_ref[...]
    ng = ng_ref[...]

    heads = range(GLA_HEADS)
    hms = [(lane >= h * GLA_DK) & (lane < (h + 1) * GLA_DK) for h in heads]

    def do_chunks(sis, row_slices):
        jobs = [(si, rows) for si in sis for rows in row_slices]
        la = [_log_sigmoid(_dot(lo_ref[si, rows, :].astype(mm), wgk) + bgk) / GLA_GATE_NORM for si, rows in jobs]
        bcum = [_dot(tri, x, HIGHEST) for x in la]
        blast = [x[chunk - 1:chunk, :] for x in bcum]
        qm, kd, kd2, vh = [], [], [], []
        for n, (si, rows) in enumerate(jobs):
            ki = k_ref[si, rows, :]
            vi = v_ref[si, rows, :]
            qd = q_ref[si, rows, :] * (GLA_DK ** -0.5) * jnp.exp(bcum[n])
            qm.append([jnp.where(hms[h], qd, 0.0).astype(mm) for h in heads])
            kd.append((ki * jnp.exp(-bcum[n])).astype(mm))
            kd2.append((ki * jnp.exp(blast[n] - bcum[n])).astype(mm))
            vh.append([vi[:, h * GLA_DV:(h + 1) * GLA_DV].astype(mm) for h in heads])
        nj = range(len(jobs))
        att = [[_dot_nt(qm[n][h], kd[n]) for h in heads] for n in nj]
        upd = [[_dot_tn(vh[n][h], kd2[n]) for h in heads] for n in nj]
        att = [[jnp.where(tri > 0.0, att[n][h], 0.0).astype(mm) for h in heads] for n in nj]
        intra = [[_dot(att[n][h], vh[n][h]) for h in heads] for n in nj]
        st = [st_ref[si] for si in sis]
        inter = [None] * len(jobs)
        for c in range(len(row_slices)):
            ns = [u * len(row_slices) + c for u in range(len(sis))]
            for u, n in enumerate(ns):
                stm = st[u].astype(mm)
                inter[n] = [_dot_nt(qm[n][h], stm) for h in heads]
            for u, n in enumerate(ns):
                new = st[u] * jnp.exp(blast[n])
                for h in heads:
                    new = new + jnp.where(hms[h], upd[n][h], 0.0)
                st[u] = new
        for u, si in enumerate(sis):
            st_ref[si] = st[u]
        for n, (si, rows) in enumerate(jobs):
            gi = gg_ref[si, rows, :]
            for h in heads:
                gh = gi[:, h * GLA_DV:(h + 1) * GLA_DV]
                o = intra[n][h] + inter[n][h]
                o_ref[si, rows, h * GLA_DV:(h + 1) * GLA_DV] = (_rms(o, ng) * (gh * _sigmoid(gh))).astype(o_ref.dtype)

    group = min(s, GLA_SEQS)
    step = min(nc, GLA_CHUNKS)

    def do_group(gi, carry):
        sis = [gi * group + u for u in range(group)]
        if nc == step:
            do_chunks(sis, [slice(c * chunk, (c + 1) * chunk) for c in range(nc)])
        else:
            def body(ci, c2):
                do_chunks(sis, [pl.ds(pl.multiple_of((ci * step + c) * chunk, chunk), chunk) for c in range(step)])
                return c2
            lax.fori_loop(0, nc // step, body, 0)
        return carry

    if s == group:
        do_group(0, 0)
    else:
        lax.fori_loop(0, s // group, do_group, 0)

    @pl.when(t == pl.num_programs(1) - 1)
    def _():
        sn_ref[...] = st_ref[...]


def _gla(q, k, v, lo, gg, wgk, bgk, ng, s0t, s, l, chunk):
    b, t, dq = q.shape
    dv = v.shape[-1]
    return pl.pallas_call(
        functools.partial(_gla_body, chunk=chunk),
        grid=(b // s, t // l),
        in_specs=[_tile_spec(s, l, dq), _tile_spec(s, l, dq), _tile_spec(s, l, dv), _tile_spec(s, l, LANES),
                  _tile_spec(s, l, dv), _const_spec(wgk.shape), _const_spec((1, dq)), _const_spec((1, GLA_DV)),
                  _seq_spec(s, GLA_DV, dq)],
        out_specs=[_tile_spec(s, l, dv), _seq_spec(s, GLA_DV, dq)],
        out_shape=[jax.ShapeDtypeStruct((b, t, dv), _act_dtype(chunk)), jax.ShapeDtypeStruct((b, GLA_DV, dq), F32)],
        scratch_shapes=[pltpu.VMEM((s, GLA_DV, dq), F32)],
        compiler_params=_params(("parallel", "arbitrary")),
        name="gla",
    )(q, k, v, lo, gg, wgk, bgk, ng, s0t)


def _lru_body(xb_ref, yb_ref, cbuf_ref, h0_ref, cw_ref, cb_ref, gw_ref, gb_ref, lam_ref,
              o_ref, ncbuf_ref, nh_ref, xbuf, hc):
    s, l, c = xb_ref.shape
    m = s * l
    nw = cw_ref.shape[0]
    t = pl.program_id(1)

    @pl.when(t == 0)
    def _():
        xbuf[:, HIST - (nw - 1):HIST, :] = cbuf_ref[...]
        hc[...] = h0_ref[...]

    @pl.when(t > 0)
    def _():
        xbuf[:, 0:HIST, :] = xbuf[:, l:l + HIST, :]

    xbuf[:, HIST:HIST + l, :] = xb_ref[...]
    xe = xbuf[...].reshape(s * (l + HIST), c)
    xc = cb_ref[...].reshape(1, 1, c)
    for j in range(nw):
        shift = nw - 1 - j
        xs = xe if shift == 0 else pltpu.roll(xe, shift, 0)
        xc = xc + xs.reshape(s, l + HIST, c)[:, HIST:, :] * cw_ref[j:j + 1, :].reshape(1, 1, c)
    xc = xc.reshape(m, c)
    gates = _dot(xc.astype(BF16), gw_ref[...]) + gb_ref[...]
    r = _sigmoid(gates[:, :c])
    i = _sigmoid(gates[:, c:])
    log_a = (-LRU_C * r) * _softplus(-lam_ref[...])
    a = jnp.exp(log_a)
    bx = jnp.sqrt(-_expm1(2.0 * log_a)) * i * xc
    a = a.reshape(m // SUBLANES, SUBLANES, c)
    bx = bx.reshape(m // SUBLANES, SUBLANES, c)
    pos = lax.broadcasted_iota(jnp.int32, a.shape, 1)
    d = 1
    while d < SUBLANES:
        valid = pos >= d
        a_prev = jnp.where(valid, pltpu.roll(a, d, 1), 1.0)
        b_prev = jnp.where(valid, pltpu.roll(bx, d, 1), 0.0)
        bx = a * b_prev + bx
        a = a * a_prev
        d *= 2
    a = a.reshape(s, l, c)
    bx = bx.reshape(s, l, c)
    h_prev = hc[...]
    groups = []
    for gi in range(l // SUBLANES):
        rows = slice(gi * SUBLANES, (gi + 1) * SUBLANES)
        hg = bx[:, rows, :] + a[:, rows, :] * h_prev
        groups.append(hg)
        h_prev = hg[:, SUBLANES - 1:SUBLANES, :]
    hs = jnp.concatenate(groups, axis=1).reshape(m, c)
    o_ref[...] = (hs * _gelu(yb_ref[...].reshape(m, c))).reshape(s, l, c).astype(o_ref.dtype)
    hc[...] = h_prev

    @pl.when(t == pl.num_programs(1) - 1)
    def _():
        ncbuf_ref[...] = xbuf[:, l + HIST - (nw - 1):l + HIST, :]
        nh_ref[...] = hc[...]


def _lru(xb, yb, cbuf, h0, cw, cb, gw, gb, lam, s, l):
    b, t, c = xb.shape
    nw = cw.shape[0]
    return pl.pallas_call(
        _lru_body,
        grid=(b // s, t // l),
        in_specs=[_tile_spec(s, l, c), _tile_spec(s, l, c), _seq_spec(s, nw - 1, c), _seq_spec(s, 1, c),
                  _const_spec(cw.shape), _const_spec((1, c)), _const_spec(gw.shape), _const_spec((1, 2 * c)),
                  _const_spec((1, c))],
        out_specs=[_tile_spec(s, l, c), _seq_spec(s, nw - 1, c), _seq_spec(s, 1, c)],
        out_shape=[jax.ShapeDtypeStruct((b, t, c), _act_dtype(l)), jax.ShapeDtypeStruct((b, nw - 1, c), F32),
                   jax.ShapeDtypeStruct((b, 1, c), F32)],
        scratch_shapes=[pltpu.VMEM((s, l + HIST, c), F32), pltpu.VMEM((s, 1, c), F32)],
        compiler_params=_params(("parallel", "arbitrary")),
        name="rg_lru",
    )(xb, yb, cbuf, h0, cw, cb, gw, gb, lam)


def _mix_ffn_body(*refs, n_in):
    a_refs, w_refs = refs[:n_in], refs[n_in:2 * n_in]
    x_ref, buf_ref, g1_ref, g2_ref, g3_ref, wup_ref, cw_ref, cb_ref, wdn_ref, y_ref, nbuf_ref, gbuf = refs[2 * n_in:]
    s, l, d = x_ref.shape
    m = s * l
    nw, f = cw_ref.shape
    t = pl.program_id(1)
    mix = None
    for a_ref, w_ref in zip(a_refs, w_refs):
        part = _dot(a_ref[...].reshape(m, a_ref.shape[-1]).astype(BF16), w_ref[...])
        mix = part if mix is None else mix + part
    x = x_ref[...].reshape(m, d) + _rms(mix, g1_ref[...])
    h = _rms(x, g2_ref[...]).astype(BF16)

    @pl.when(t == 0)
    def _():
        gbuf[:, HIST - (nw - 1):HIST, :] = buf_ref[...]

    @pl.when(t > 0)
    def _():
        gbuf[:, 0:HIST, :] = gbuf[:, l:l + HIST, :]

    gbuf[:, HIST:HIST + l, :] = _dot(h, wup_ref[:, :f]).reshape(s, l, f)
    u = _dot(h, wup_ref[:, f:])
    gc = cb_ref[...].reshape(1, 1, f)
    for j in range(nw):
        off = HIST - (nw - 1) + j
        gc = gc + gbuf[:, off:off + l, :] * cw_ref[j:j + 1, :].reshape(1, 1, f)
    act = (_gelu(gc).reshape(m, f) * u).astype(BF16)
    y = _dot(act, wdn_ref[...])
    y_ref[...] = (x + _rms(y, g3_ref[...])).reshape(s, l, d)

    @pl.when(t == pl.num_programs(1) - 1)
    def _():
        nbuf_ref[...] = gbuf[:, l + HIST - (nw - 1):l + HIST, :]


def _mix_ffn(acts, ws, x, buf, g1, g2, g3, wup, cw, cb, wdn, s, l):
    b, t, d = x.shape
    nw, f = cw.shape

    def resident(w):
        return pl.BlockSpec(w.shape, lambda b_, t_: (0, 0), pipeline_mode=pl.Buffered(1))

    return pl.pallas_call(
        functools.partial(_mix_ffn_body, n_in=len(acts)),
        grid=(b // s, t // l),
        in_specs=[_tile_spec(s, l, a.shape[-1]) for a in acts] + [resident(w) for w in ws]
                 + [_tile_spec(s, l, d), _seq_spec(s, nw - 1, f), _const_spec((1, d)), _const_spec((1, d)),
                    _const_spec((1, d)), resident(wup), _const_spec(cw.shape), _const_spec((1, f)), resident(wdn)],
        out_specs=[_tile_spec(s, l, d), _seq_spec(s, nw - 1, f)],
        out_shape=[jax.ShapeDtypeStruct((b, t, d), F32), jax.ShapeDtypeStruct((b, nw - 1, f), F32)],
        scratch_shapes=[pltpu.VMEM((s, l + HIST, f), F32)],
        compiler_params=_params(("parallel", "arbitrary")),
        name="mix_ffn",
    )(*acts, *ws, x, buf, g1, g2, g3, wup, cw, cb, wdn)


def _fox_prompt_body(qt_ref, ct_ref, kaug_ref, vaug_ref, o_ref, m_ref, acc_ref, sa_ref, sb_ref, cmax_a_ref, cmax_b_ref,
                     *, tq):
    hp = pl.program_id(1)
    iq = pl.program_id(2)
    qt = qt_ref[0]
    feat = lax.broadcasted_iota(jnp.int32, (LANES, 1), 0)
    qas = []
    for j in range(2):
        head = 2 * hp + j
        hi, mid, lo = _split3(ct_ref[0, pl.ds(head, 1), :])
        pick = jnp.where(((feat & (FOX_HEADS - 1)) == head) & (feat < 3 * FOX_HEADS), 1.0, 0.0)
        augt = jnp.where(feat == 3 * FOX_HEADS, hi,
                         jnp.where(feat == 3 * FOX_HEADS + 1, mid, jnp.where(feat == 3 * FOX_HEADS + 2, lo, pick)))
        own = (feat < FOX_HD) if j == 0 else (feat >= FOX_HD)
        qas.append(jnp.concatenate([jnp.where(own, qt, jnp.zeros_like(qt)), augt.astype(BF16)], axis=0))
    qa = jnp.concatenate(qas, axis=1)
    m_ref[...] = jnp.full_like(m_ref, NEG_INF)
    acc_ref[...] = jnp.zeros_like(acc_ref)
    key = lax.broadcasted_iota(jnp.int32, (tq, tq), 0)
    qry = lax.broadcasted_iota(jnp.int32, (tq, tq), 1)

    def logits(ik, buf):
        s_ref, cmax_ref = buf
        kt = kaug_ref[0, 0, pl.ds(pl.multiple_of(ik * tq, tq), tq), :]
        s2 = _dot(kt, qa)
        s_ref[...] = s2
        cmax_ref[...] = jnp.max(s2, axis=0, keepdims=True)

    def update(ik, buf, masked):
        s_ref, cmax_ref = buf
        for j in range(2):
            st = s_ref[:, j * tq:(j + 1) * tq]
            if masked:
                st = jnp.where(key <= qry, st, NEG_INF)
                tile_max = jnp.max(st, axis=0, keepdims=True)
            else:
                tile_max = cmax_ref[:, j * tq:(j + 1) * tq]
            m_old = m_ref[j]
            m_new = jnp.maximum(m_old, tile_max)
            pt = jnp.exp2(st - m_new).astype(BF16)
            m_ref[j] = m_new
            acc_ref[j] = acc_ref[j] * jnp.exp2(m_old - m_new) + _dot(vaug_ref[0, j, ik], pt)

    buf_a = (sa_ref, cmax_a_ref)
    buf_b = (sb_ref, cmax_b_ref)
    logits(0, buf_a)

    def pair_body(i, carry):
        logits(2 * i + 1, buf_b)
        update(2 * i, buf_a, False)
        logits(2 * i + 2, buf_a)
        update(2 * i + 1, buf_b, False)
        return carry

    lax.fori_loop(0, iq >> 1, pair_body, 0)

    @pl.when((iq & 1) == 0)
    def _():
        update(iq, buf_a, True)

    @pl.when((iq & 1) == 1)
    def _():
        logits(iq, buf_b)
        update(iq - 1, buf_a, False)
        update(iq, buf_b, True)

    a0 = acc_ref[0]
    a1 = acc_ref[1]
    ot = jnp.concatenate([a0[:FOX_HD] / a0[FOX_HD:], a1[FOX_HD:] / a1[:FOX_HD]], axis=0)
    o_ref[0] = ot.T.astype(o_ref.dtype)


def _fox_prompt(qt, ct2, kaug, vaug, tq):
    b, da, t = qt.shape
    nq = t // tq
    assert vaug.shape[2] == nq and vaug.shape[-1] == tq
    return pl.pallas_call(
        functools.partial(_fox_prompt_body, tq=tq),
        grid=(b, FOX_HEADS // 2, nq),
        in_specs=[pl.BlockSpec((1, LANES, tq), lambda b_, h_, i_: (b_, h_, i_)),
                  pl.BlockSpec((1, FOX_HEADS, tq), lambda b_, h_, i_: (b_, 0, i_)),
                  pl.BlockSpec((1, 1) + kaug.shape[2:], lambda b_, h_, i_: (b_, h_, 0, 0)),
                  pl.BlockSpec((1, 2) + vaug.shape[2:], lambda b_, h_, i_: (b_, h_, 0, 0, 0))],
        out_specs=pl.BlockSpec((1, tq, LANES), lambda b_, h_, i_: (b_, i_, h_)),
        out_shape=jax.ShapeDtypeStruct((b, t, da), BF16),
        scratch_shapes=[pltpu.VMEM((2, 1, tq), F32), pltpu.VMEM((2, LANES, tq), F32),
                        pltpu.VMEM((tq, 2 * tq), F32), pltpu.VMEM((tq, 2 * tq), F32),
                        pltpu.VMEM((1, 2 * tq), F32), pltpu.VMEM((1, 2 * tq), F32)],
        compiler_params=_params(("parallel", "parallel", "arbitrary")),
        name="fox_prompt",
    )(qt, ct2, kaug, vaug)


def _fox_sample_body(pt_ref, q_ref, kn_ref, vn_ref, cnt_ref, cnr_ref, *rest, npp):
    kp = rest[0:npp]
    vp = rest[npp:2 * npp]
    lp = rest[2 * npp:3 * npp]
    o_ref, m_ref, l_ref, acc_ref, carry_ref, kb_ref, vb_ref = rest[3 * npp:]
    del pt_ref
    j = pl.program_id(1)
    t, da = q_ref.shape[1], q_ref.shape[2]
    rows = FOX_HEADS * t
    row_c = lax.broadcasted_iota(jnp.int32, (rows, da), 0)
    col_c = lax.broadcasted_iota(jnp.int32, (rows, da), 1)
    own = (col_c & -FOX_HD) == (row_c & -t) * (FOX_HD // t)
    q8 = q_ref[0] * FOX_SCALE
    qbd = jnp.where(own, jnp.concatenate([q8] * FOX_HEADS, axis=0), 0.0).astype(BF16)
    cn_rows = cnr_ref[0]

    def head_rows(x):
        return jnp.concatenate([jnp.broadcast_to(x[h:h + 1, :], (t, x.shape[1])) for h in range(FOX_HEADS)], axis=0)

    @pl.when(j == 0)
    def _():
        pad = jnp.zeros((PAGE_SIZE - t, da), F32)
        kn = jnp.concatenate([kn_ref[0], pad], axis=0).astype(BF16)
        vn = jnp.concatenate([vn_ref[0], pad], axis=0).astype(BF16)
        row_l = lax.broadcasted_iota(jnp.int32, (rows, PAGE_SIZE), 0)
        col_l = lax.broadcasted_iota(jnp.int32, (rows, PAGE_SIZE), 1)
        sc = _dot_nt(qbd, kn) + cn_rows - head_rows(cnt_ref[0])
        sc = jnp.where(col_l <= (row_l & (t - 1)), sc, NEG_INF)
        m0 = jnp.max(sc, axis=-1, keepdims=True)
        p = jnp.exp(sc - m0)
        m_ref[...] = m0
        l_ref[...] = jnp.sum(p, axis=-1, keepdims=True)
        acc_ref[...] = _dot(p.astype(BF16), vn)
        carry_ref[...] = jnp.zeros_like(carry_ref)

    prow = lax.broadcasted_iota(jnp.int32, (PAGE_SIZE, PAGE_SIZE), 0)
    pcol = lax.broadcasted_iota(jnp.int32, (PAGE_SIZE, PAGE_SIZE), 1)
    later = jnp.where(prow > pcol, 1.0, 0.0).astype(F32)
    for i in range(npp):
        kb_ref[:, i * PAGE_SIZE:(i + 1) * PAGE_SIZE] = kp[i][...].astype(BF16)
        vb_ref[:, i * PAGE_SIZE:(i + 1) * PAGE_SIZE] = vp[i][...].astype(BF16)
    lf_all = jnp.concatenate([lp[i][...] for i in range(npp)], axis=0)
    inside = _dot(lf_all, later, HIGHEST)
    biases = []
    carry = carry_ref[...]
    for i in range(npp):
        rows_i = slice(i * FOX_HEADS, (i + 1) * FOX_HEADS)
        biases.append(head_rows(inside[rows_i] + carry))
        carry = carry + (inside[rows_i, 0:1] + lf_all[rows_i, 0:1])
    carry_ref[...] = carry
    sc = _dot(qbd, kb_ref[...]) + cn_rows + jnp.concatenate(biases, axis=1)
    m_old = m_ref[...]
    m_new = jnp.maximum(m_old, jnp.max(sc, axis=-1, keepdims=True))
    alpha = jnp.exp(m_old - m_new)
    p = jnp.exp(sc - m_new)
    l_new = alpha * l_ref[...] + jnp.sum(p, axis=-1, keepdims=True)
    acc = acc_ref[...] * alpha + _dot_nt(p.astype(BF16), vb_ref[...])
    m_ref[...] = m_new
    l_ref[...] = l_new
    acc_ref[...] = acc

    @pl.when(j == pl.num_programs(1) - 1)
    def _():
        full = jnp.where(own, acc / l_new, 0.0)
        out = full[0:t, :]
        for h in range(1, FOX_HEADS):
            out = out + full[h * t:(h + 1) * t, :]
        o_ref[0] = out


def _fox_sample(q, k, v, c, cache_k, cache_v, cache_logf, page_table, layer):
    b, t, da = q.shape
    n_pages = page_table.shape[1]
    npp = PAGES_PER_STEP
    assert n_pages % npp == 0 and PAGE_SIZE % t == 0 and t % SUBLANES == 0
    n_pool, n_layers = cache_k.shape[0], cache_k.shape[1]
    ckt = cache_k.transpose(0, 1, 3, 4, 2).reshape(n_pool, n_layers, da, PAGE_SIZE)
    cvt = cache_v.transpose(0, 1, 3, 4, 2).reshape(n_pool, n_layers, da, PAGE_SIZE)
    clt = cache_logf.transpose(0, 1, 3, 2)
    ct = c.transpose(0, 2, 1)
    cn_t = jnp.pad(ct, ((0, 0), (0, 0), (0, PAGE_SIZE - t)))
    cn_rows = ct.reshape(b, FOX_HEADS * t, 1)

    def page_spec(i, height):
        return pl.BlockSpec((None, None, height, PAGE_SIZE),
                            lambda b_, j_, pt: (pt[b_, n_pages - 1 - (j_ * npp + i)], layer, 0, 0))

    tok = pl.BlockSpec((1, t, da), lambda b_, j_, pt: (b_, 0, 0))
    rows = FOX_HEADS * t
    grid_spec = pltpu.PrefetchScalarGridSpec(
        num_scalar_prefetch=1,
        grid=(b, n_pages // npp),
        in_specs=[tok, tok, tok,
                  pl.BlockSpec((1, FOX_HEADS, PAGE_SIZE), lambda b_, j_, pt: (b_, 0, 0)),
                  pl.BlockSpec((1, rows, 1), lambda b_, j_, pt: (b_, 0, 0))]
                 + [page_spec(i, da) for i in range(npp)] * 2
                 + [page_spec(i, FOX_HEADS) for i in range(npp)],
        out_specs=tok,
        scratch_shapes=[pltpu.VMEM((rows, 1), F32), pltpu.VMEM((rows, 1), F32), pltpu.VMEM((rows, da), F32),
                        pltpu.VMEM((FOX_HEADS, 1), F32),
                        pltpu.VMEM((da, npp * PAGE_SIZE), BF16), pltpu.VMEM((da, npp * PAGE_SIZE), BF16)],
    )
    return pl.pallas_call(
        functools.partial(_fox_sample_body, npp=npp),
        grid_spec=grid_spec,
        out_shape=jax.ShapeDtypeStruct((b, t, da), F32),
        compiler_params=_params(("parallel", "arbitrary")),
        name="fox_sample",
    )(page_table, q, k, v, cn_t, cn_rows, *([ckt] * npp), *([cvt] * npp), *([clt] * npp))


def _pow2_tile(t, want):
    l = min(t, want)
    assert t % l == 0 and l & (l - 1) == 0
    return l


def _prep_weights(p):
    w = {}
    n_a = p['w_in_a'].shape[0]
    wa = p['w_in_a']
    d = wa.shape[1]
    q_end = 2 * GLA_HEADS * GLA_DK + GLA_HEADS * GLA_DV
    rank = p['w_gk2'].shape[1]
    w['w_in_a'] = jnp.concatenate(
        [wa[:, :, :q_end], wa[:, :, q_end + rank:], wa[:, :, q_end:q_end + rank],
         jnp.zeros((n_a, d, LANES - rank), wa.dtype)], axis=-1).astype(BF16)
    w['w_gk2'] = jnp.pad(p['w_gk2'], ((0, 0), (0, LANES - rank), (0, 0)))
    gw = p['lru_gate_w']
    nb, bw = gw.shape[2], gw.shape[3]
    eye = jnp.eye(nb, dtype=gw.dtype)
    dense = jnp.einsum('agncd,nm->agncmd', gw, eye).reshape(n_a, 2, nb * bw, nb * bw)
    w['lru_gate_w'] = jnp.concatenate([dense[:, 0], dense[:, 1]], axis=-1).astype(BF16)
    w['lru_gate_b'] = p['lru_gate_b'].reshape(n_a, 1, -1)
    w['w_out_a'] = p['w_out_a'].astype(BF16)
    wc = p['w_in_c']
    n_c = wc.shape[0]
    da = FOX_HEADS * FOX_HD
    w['w_in_c'] = jnp.concatenate([wc, jnp.zeros((n_c, d, LANES - FOX_HEADS), wc.dtype)], axis=-1).astype(BF16)
    w['b_f'] = jnp.pad(p['b_f'], ((0, 0), (0, LANES - FOX_HEADS))).reshape(n_c, 1, LANES)
    w['w_qkvf_t'] = jnp.swapaxes(w['w_in_c'], 1, 2)
    w['b_f_col'] = w['b_f'].reshape(n_c, LANES, 1)
    w['eye'] = jnp.eye(LANES, dtype=BF16)
    w['w_out_c'] = p['w_out_c'].astype(BF16)
    w['ffn_w_up'] = p['ffn_w_up'].astype(BF16)
    w['ffn_w_down'] = p['ffn_w_down'].astype(BF16)
    return w


def _trunk(x, gla_s, lru_buf, lru_h, ffn_buf, layer_c, p, w, s, tiles):
    b, t, d = x.shape
    depth = p['norm_g'].shape[0]
    l_proj, l_gla, l_lru, l_ffn = (_pow2_tile(t, n) for n in tiles)
    chunk = np.gcd(t, GLA_CHUNK).item()
    a_widths = [GLA_HEADS * GLA_DK] * 2 + [GLA_HEADS * GLA_DV] * 2 + [lru_buf.shape[-1]] * 2 + [LANES]
    w_split = GLA_HEADS * GLA_DV
    gla_l, buf_l, h_l, k_l, v_l, lf_l, ffn_l = [], [], [], [], [], [], []
    ia = ic = 0
    for layer in range(depth):
        ng = p['norm_g'][layer].reshape(4, 1, d)
        if layer % 2 == 0:
            q, k, v, gg, xb, yb, lo = _norm_proj(x, ng[0], w['w_in_a'][ia], a_widths, s, l_proj)
            s0t = gla_s[:, ia].reshape(b, GLA_HEADS * GLA_DK, GLA_DV).transpose(0, 2, 1)
            o, snt = _gla(q, k, v, lo, gg, w['w_gk2'][ia], p['b_gk2'][ia].reshape(1, -1),
                          p['gla_norm_g'][ia].reshape(1, -1), s0t, s if s > 1 else min(b, GLA_SEQS), l_gla, chunk)
            lru_o, nbuf, nh = _lru(xb, yb, lru_buf[:, ia], lru_h[:, ia][:, None, :], p['lru_conv_w'][ia],
                                   p['lru_conv_b'][ia].reshape(1, -1), w['lru_gate_w'][ia], w['lru_gate_b'][ia],
                                   p['lru_lambda'][ia].reshape(1, -1), s, l_lru)
            acts, ws = [o, lru_o], [w['w_out_a'][ia][:w_split], w['w_out_a'][ia][w_split:]]
            gla_l.append(snt.transpose(0, 2, 1).reshape(b, GLA_HEADS, GLA_DK, GLA_DV))
            buf_l.append(nbuf)
            h_l.append(nh[:, 0, :])
            ia += 1
        else:
            o, k_new, v_new, lf_new = layer_c(x, ng[0], ic)
            acts, ws = [o], [w['w_out_c'][ic]]
            k_l.append(k_new)
            v_l.append(v_new)
            lf_l.append(lf_new)
            ic += 1
        x, fb = _mix_ffn(acts, ws, x, ffn_buf[:, layer], ng[1], ng[2], ng[3], w['ffn_w_up'][layer],
                         p['ffn_conv_w'][layer], p['ffn_conv_b'][layer].reshape(1, -1), w['ffn_w_down'][layer], s, l_ffn)
        ffn_l.append(fb)
    return (x, jnp.stack(gla_l, axis=1), jnp.stack(buf_l, axis=1), jnp.stack(h_l, axis=1), jnp.stack(k_l, axis=1),
            jnp.stack(v_l, axis=1), jnp.stack(lf_l, axis=1), jnp.stack(ffn_l, axis=1))


PROMPT_TILES = (512, 512, 256, 512)
FOX_TQ = 512


def kernel(x_prompt, x_sample, state_gla, state_lru_conv, state_lru_h, cache_k, cache_v, cache_logf, state_ffn_conv, page_table, norm_g, w_in_a, w_gk2, b_gk2, gla_norm_g, lru_conv_w, lru_conv_b, lru_gate_w, lru_gate_b, lru_lambda, w_out_a, w_in_c, b_f, w_out_c, ffn_w_up, ffn_conv_w, ffn_conv_b, ffn_w_down):
    p = {'norm_g': norm_g, 'w_in_a': w_in_a, 'w_gk2': w_gk2, 'b_gk2': b_gk2, 'gla_norm_g': gla_norm_g,
         'lru_conv_w': lru_conv_w, 'lru_conv_b': lru_conv_b, 'lru_gate_w': lru_gate_w, 'lru_gate_b': lru_gate_b,
         'lru_lambda': lru_lambda, 'w_out_a': w_out_a, 'w_in_c': w_in_c, 'b_f': b_f, 'w_out_c': w_out_c,
         'ffn_w_up': ffn_w_up, 'ffn_conv_w': ffn_conv_w, 'ffn_conv_b': ffn_conv_b, 'ffn_w_down': ffn_w_down}
    w = _prep_weights(p)
    bp = x_prompt.shape[0]
    bs, ts, _ = x_sample.shape
    dt = x_prompt.dtype
    gla0 = jnp.zeros((bp,) + state_gla.shape[1:], dt)
    lru_buf0 = jnp.zeros((bp,) + state_lru_conv.shape[1:], dt)
    lru_h0 = jnp.zeros((bp,) + state_lru_h.shape[1:], dt)
    ffn_buf0 = jnp.zeros((bp,) + state_ffn_conv.shape[1:], dt)

    def layer_c_prompt(x, g, ic):
        b, t, _ = x.shape
        l = _pow2_tile(t, FOX_TQ)
        qt, kt, vt, lft, ct2, kaug, vaug = _proj_c_prompt(x, g, w['w_qkvf_t'][ic], w['b_f_col'][ic], w['eye'], l)
        o = _fox_prompt(qt, ct2, kaug, vaug, l)
        k_new = kt.reshape(b, FOX_HEADS, FOX_HD, t).transpose(0, 3, 1, 2)
        v_new = vt.reshape(b, FOX_HEADS, FOX_HD, t).transpose(0, 3, 1, 2)
        return o, k_new, v_new, lft.transpose(0, 2, 1)

    def layer_c_sample(x, g, ic):
        b, t, _ = x.shape
        q, k, v, lf, c = _proj_c_sample(x, g, w['w_in_c'][ic], w['b_f'][ic])
        o = _fox_sample(q, k, v, c, cache_k, cache_v, cache_logf, page_table, ic)
        return o, k.reshape(b, t, FOX_HEADS, FOX_HD), v.reshape(b, t, FOX_HEADS, FOX_HD), lf

    outs_p = _trunk(x_prompt, gla0, lru_buf0, lru_h0, ffn_buf0, layer_c_prompt, p, w, 1, PROMPT_TILES)
    outs_s = _trunk(x_sample, state_gla, state_lru_conv, state_lru_h, state_ffn_conv, layer_c_sample, p, w, bs,
                    (ts, ts, ts, ts))
    return tuple(o for pair in zip(outs_p, outs_s) for o in pair)
```

```python
import functools

import numpy as np
import jax
import jax.numpy as jnp
from jax import lax
from jax.experimental import pallas as pl
from jax.experimental.pallas import tpu as pltpu

F32 = jnp.float32
BF16 = jnp.bfloat16
HIGHEST = lax.Precision.HIGHEST

EPS = 1e-6
GLA_HEADS = 4
GLA_DK = 64
GLA_DV = 128
GLA_GATE_NORM = 16.0
GLA_CHUNK = 64
LRU_C = 8.0
FOX_HEADS = 16
FOX_HD = 64
FOX_SCALE = FOX_HD ** -0.5
PAGE_SIZE = 128
LANES = 128
SUBLANES = 8
HIST = SUBLANES
VMEM_LIMIT = 56 * 2 ** 20
PAGES_PER_STEP = 16
GLA_SEQS = 2
GLA_CHUNKS = 4
NEG_INF = float("-inf")
LOG2E = 1.4426950408889634


def _rms(x, g):
    return x * lax.rsqrt(jnp.mean(x * x, axis=-1, keepdims=True) + EPS) * g


def _sigmoid(x):
    return 0.5 * (1.0 + jnp.tanh(0.5 * x))


def _softplus(x):
    return jnp.maximum(x, 0.0) + jnp.log1p(jnp.exp(-jnp.abs(x)))


def _log_sigmoid(x):
    return -_softplus(-x)


def _expm1(x):
    return jnp.tanh(0.5 * x) * (jnp.exp(x) + 1.0)


def _gelu(x):
    return x * (0.5 * (1.0 + jnp.tanh(0.7978845608028654 * (x + 0.044715 * (x * x * x)))))


def _dot(a, b, precision=None):
    return jnp.dot(a, b, preferred_element_type=F32, precision=precision)


def _dot_nt(a, b, precision=None):
    return lax.dot_general(a, b, (((1,), (1,)), ((), ())), preferred_element_type=F32, precision=precision)


def _dot_tn(a, b, precision=None):
    return lax.dot_general(a, b, (((0,), (0,)), ((), ())), preferred_element_type=F32, precision=precision)


def _seq_tril(m, l):
    row = lax.broadcasted_iota(jnp.int32, (m, m), 0)
    col = lax.broadcasted_iota(jnp.int32, (m, m), 1)
    same = (row & -l) == (col & -l)
    return jnp.where(same & (col <= row), 1.0, 0.0).astype(F32)


def _split3(x):
    hi = x.astype(BF16).astype(F32)
    mid = (x - hi).astype(BF16).astype(F32)
    lo = (x - hi - mid).astype(BF16).astype(F32)
    return hi, mid, lo


def _act_dtype(rows):
    return BF16 if rows % (2 * SUBLANES) == 0 else F32


def _params(sem):
    return pltpu.CompilerParams(dimension_semantics=sem, vmem_limit_bytes=VMEM_LIMIT)


def _tile_spec(s, l, n):
    return pl.BlockSpec((s, l, n), lambda b, t: (b, t, 0))


def _seq_spec(s, r, n):
    return pl.BlockSpec((s, r, n), lambda b, t: (b, 0, 0))


def _const_spec(shape):
    return pl.BlockSpec(shape, lambda b, t: (0,) * len(shape))


def _norm_proj_body(x_ref, g_ref, w_ref, *o_refs):
    s, l, d = x_ref.shape
    h = _rms(x_ref[...].reshape(s * l, d), g_ref[...]).astype(BF16)
    off = 0
    for o_ref in o_refs:
        n = o_ref.shape[-1]
        o_ref[...] = _dot(h, w_ref[:, off:off + n]).reshape(s, l, n).astype(o_ref.dtype)
        off += n


def _norm_proj(x, g, w, widths, s, l):
    b, t, d = x.shape
    return pl.pallas_call(
        _norm_proj_body,
        grid=(b // s, t // l),
        in_specs=[_tile_spec(s, l, d), _const_spec((1, d)), _const_spec(w.shape)],
        out_specs=[_tile_spec(s, l, n) for n in widths],
        out_shape=[jax.ShapeDtypeStruct((b, t, n), F32) for n in widths],
        compiler_params=_params(("parallel", "parallel")),
        name="norm_proj_a",
    )(x, g, w)


def _proj_cs_body(x_ref, g_ref, w_ref, bf_ref, q_ref, k_ref, v_ref, lf_ref, c_ref):
    s, l, d = x_ref.shape
    m = s * l
    da = q_ref.shape[-1]
    nh = lf_ref.shape[-1]
    h = _rms(x_ref[...].reshape(m, d), g_ref[...]).astype(BF16)
    q_ref[...] = _dot(h, w_ref[:, 0:da]).reshape(s, l, da)
    k_ref[...] = _dot(h, w_ref[:, da:2 * da]).reshape(s, l, da)
    v_ref[...] = _dot(h, w_ref[:, 2 * da:3 * da]).reshape(s, l, da)
    lf = _log_sigmoid(_dot(h, w_ref[:, 3 * da:]) + bf_ref[...])
    c = _dot(_seq_tril(m, l), lf, HIGHEST)
    lf_ref[...] = lf[:, :nh].reshape(s, l, nh)
    c_ref[...] = c[:, :nh].reshape(s, l, nh)


def _proj_c_sample(x, g, w, bf):
    b, t, d = x.shape
    da = FOX_HEADS * FOX_HD
    wide = jax.ShapeDtypeStruct((b, t, da), F32)
    narrow = jax.ShapeDtypeStruct((b, t, FOX_HEADS), F32)
    return pl.pallas_call(
        _proj_cs_body,
        grid=(1, 1),
        in_specs=[_tile_spec(b, t, d), _const_spec((1, d)), _const_spec(w.shape), _const_spec((1, LANES))],
        out_specs=[_tile_spec(b, t, da)] * 3 + [_tile_spec(b, t, FOX_HEADS)] * 2,
        out_shape=[wide, wide, wide, narrow, narrow],
        compiler_params=_params(("parallel", "arbitrary")),
        name="norm_proj_c_sample",
    )(x, g, w, bf)


def _proj_cp_body(x_ref, g_ref, wt_ref, bfc_ref, eye_ref,
                  qt_ref, kt_ref, vt_ref, lft_ref, ct2_ref, kaug_ref, vaug_ref, carry_ref):
    _, l, d = x_ref.shape
    da = kt_ref.shape[1]
    h = _rms(x_ref[0], g_ref[...]).astype(BF16)
    allt = _dot_nt(wt_ref[...], h)
    qt_ref[0] = (allt[0:da] * (FOX_SCALE * LOG2E)).astype(qt_ref.dtype)
    kt = allt[da:2 * da]
    kt_ref[0] = kt
    vt = allt[2 * da:3 * da]
    vt_ref[0] = vt
    lft = _log_sigmoid(allt[3 * da:] + bfc_ref[...])
    lft_ref[0] = lft[:FOX_HEADS]

    @pl.when(pl.program_id(1) == 0)
    def _():
        carry_ref[...] = jnp.zeros_like(carry_ref)

    row = lax.broadcasted_iota(jnp.int32, (l, l), 0)
    col = lax.broadcasted_iota(jnp.int32, (l, l), 1)
    ct = _dot(lft, jnp.where(row <= col, 1.0, 0.0).astype(F32), HIGHEST) + carry_ref[...]
    carry_ref[...] = ct[:, l - 1:l]
    ct2 = ct[:FOX_HEADS] * LOG2E
    ct2_ref[0] = ct2
    hi, mid, lo = _split3(ct2)
    ones_rows = jnp.where(lax.broadcasted_iota(jnp.int32, (FOX_HEADS, l), 0) < 3, 1.0, 0.0)
    augt = jnp.concatenate([-hi, -mid, -lo, ones_rows, jnp.zeros((LANES - 4 * FOX_HEADS, l), F32)], axis=0).astype(BF16)
    aug = _dot_tn(augt, eye_ref[...]).astype(BF16)
    pair = 2 * FOX_HD
    for p in range(FOX_HEADS // 2):
        kaug_ref[0, p] = jnp.concatenate([kt[p * pair:(p + 1) * pair].T.astype(BF16), aug], axis=1)
    vtb = vt.astype(BF16)
    ones = jnp.ones((FOX_HD, l), BF16)
    for hh in range(FOX_HEADS):
        vh = vtb[hh * FOX_HD:(hh + 1) * FOX_HD]
        vaug_ref[0, hh, 0] = jnp.concatenate([vh, ones] if hh % 2 == 0 else [ones, vh], axis=0)


def _proj_c_prompt(x, g, wt, bfc, eye, l):
    b, t, d = x.shape
    da = FOX_HEADS * FOX_HD
    nk = t // l
    pair = 2 * FOX_HD
    feat_t = pl.BlockSpec((1, da, l), lambda b_, t_: (b_, 0, t_))
    head_t = pl.BlockSpec((1, FOX_HEADS, l), lambda b_, t_: (b_, 0, t_))
    return pl.pallas_call(
        _proj_cp_body,
        grid=(b, nk),
        in_specs=[_tile_spec(1, l, d), _const_spec((1, d)), _const_spec(wt.shape),
                  _const_spec((LANES, 1)), _const_spec((LANES, LANES))],
        out_specs=[feat_t, feat_t, feat_t, head_t, head_t,
                   pl.BlockSpec((1, FOX_HEADS // 2, l, pair + LANES), lambda b_, t_: (b_, 0, t_, 0)),
                   pl.BlockSpec((1, FOX_HEADS, 1, pair, l), lambda b_, t_: (b_, 0, t_, 0, 0))],
        out_shape=[jax.ShapeDtypeStruct((b, da, t), BF16),
                   jax.ShapeDtypeStruct((b, da, t), F32),
                   jax.ShapeDtypeStruct((b, da, t), F32),
                   jax.ShapeDtypeStruct((b, FOX_HEADS, t), F32),
                   jax.ShapeDtypeStruct((b, FOX_HEADS, t), F32),
                   jax.ShapeDtypeStruct((b, FOX_HEADS // 2, t, pair + LANES), BF16),
                   jax.ShapeDtypeStruct((b, FOX_HEADS, nk, pair, l), BF16)],
        scratch_shapes=[pltpu.VMEM((LANES, 1), F32)],
        compiler_params=_params(("parallel", "arbitrary")),
        name="norm_proj_c_prompt",
    )(x, g, wt, bfc, eye)


def _gla_body(q_ref, k_ref, v_ref, lo_ref, gg_ref, wgk_ref, bgk_ref, ng_ref, s0_ref, o_ref, sn_ref, st_ref, *, chunk):
    s, l, dq = q_ref.shape
    nc = l // chunk
    mm = BF16 if chunk >= 16 else F32
    t = pl.program_id(1)

    @pl.when(t == 0)
    def _():
        st_ref[...] = s0_ref[...]

    tri = _seq_tril(chunk, chunk)
    lane = lax.broadcasted_iota(jnp.int32, (1, dq), 1)
    wgk = wgk_ref[...].astype(mm)
    bgk = bgk_ref[...]
    ng = ng_ref[...]

    heads = range(GLA_HEADS)
    hms = [(lane >= h * GLA_DK) & (lane < (h + 1) * GLA_DK) for h in heads]

    def do_chunks(sis, row_slices):
        jobs = [(si, rows) for si in sis for rows in row_slices]
        la = [_log_sigmoid(_dot(lo_ref[si, rows, :].astype(mm), wgk) + bgk) / GLA_GATE_NORM for si, rows in jobs]
        bcum = [_dot(tri, x, HIGHEST) for x in la]
        blast = [x[chunk - 1:chunk, :] for x in bcum]
        qm, kd, kd2, vh = [], [], [], []
        for n, (si, rows) in enumerate(jobs):
            ki = k_ref[si, rows, :]
            vi = v_ref[si, rows, :]
            qd = q_ref[si, rows, :] * (GLA_DK ** -0.5) * jnp.exp(bcum[n])
            qm.append([jnp.where(hms[h], qd, 0.0).astype(mm) for h in heads])
            kd.append((ki * jnp.exp(-bcum[n])).astype(mm))
            kd2.append((ki * jnp.exp(blast[n] - bcum[n])).astype(mm))
            vh.append([vi[:, h * GLA_DV:(h + 1) * GLA_DV].astype(mm) for h in heads])
        nj = range(len(jobs))
        att = [[_dot_nt(qm[n][h], kd[n]) for h in heads] for n in nj]
        upd = [[_dot_tn(vh[n][h], kd2[n]) for h in heads] for n in nj]
        att = [[jnp.where(tri > 0.0, att[n][h], 0.0).astype(mm) for h in heads] for n in nj]
        intra = [[_dot(att[n][h], vh[n][h]) for h in heads] for n in nj]
        st = [st_ref[si] for si in sis]
        inter = [None] * len(jobs)
        for c in range(len(row_slices)):
            ns = [u * len(row_slices) + c for u in range(len(sis))]
            for u, n in enumerate(ns):
                stm = st[u].astype(mm)
                inter[n] = [_dot_nt(qm[n][h], stm) for h in heads]
            for u, n in enumerate(ns):
                new = st[u] * jnp.exp(blast[n])
                for h in heads:
                    new = new + jnp.where(hms[h], upd[n][h], 0.0)
                st[u] = new
        for u, si in enumerate(sis):
            st_ref[si] = st[u]
        for n, (si, rows) in enumerate(jobs):
            gi = gg_ref[si, rows, :]
            for h in heads:
                gh = gi[:, h * GLA_DV:(h + 1) * GLA_DV]
                o = intra[n][h] + inter[n][h]
                o_ref[si, rows, h * GLA_DV:(h + 1) * GLA_DV] = (_rms(o, ng) * (gh * _sigmoid(gh))).astype(o_ref.dtype)

    group = min(s, GLA_SEQS)
    step = min(nc, GLA_CHUNKS)

    def do_group(gi, carry):
        sis = [gi * group + u for u in range(group)]
        if nc == step:
            do_chunks(sis, [slice(c * chunk, (c + 1) * chunk) for c in range(nc)])
        else:
            def body(ci, c2):
                do_chunks(sis, [pl.ds(pl.multiple_of((ci * step + c) * chunk, chunk), chunk) for c in range(step)])
                return c2
            lax.fori_loop(0, nc // step, body, 0)
        return carry

    if s == group:
        do_group(0, 0)
    else:
        lax.fori_loop(0, s // group, do_group, 0)

    @pl.when(t == pl.num_programs(1) - 1)
    def _():
        sn_ref[...] = st_ref[...]


def _gla(q, k, v, lo, gg, wgk, bgk, ng, s0t, s, l, chunk):
    b, t, dq = q.shape
    dv = v.shape[-1]
    return pl.pallas_call(
        functools.partial(_gla_body, chunk=chunk),
        grid=(b // s, t // l),
        in_specs=[_tile_spec(s, l, dq), _tile_spec(s, l, dq), _tile_spec(s, l, dv), _tile_spec(s, l, LANES),
                  _tile_spec(s, l, dv), _const_spec(wgk.shape), _const_spec((1, dq)), _const_spec((1, GLA_DV)),
                  _seq_spec(s, GLA_DV, dq)],
        out_specs=[_tile_spec(s, l, dv), _seq_spec(s, GLA_DV, dq)],
        out_shape=[jax.ShapeDtypeStruct((b, t, dv), _act_dtype(chunk)), jax.ShapeDtypeStruct((b, GLA_DV, dq), F32)],
        scratch_shapes=[pltpu.VMEM((s, GLA_DV, dq), F32)],
        compiler_params=_params(("parallel", "arbitrary")),
        name="gla",
    )(q, k, v, lo, gg, wgk, bgk, ng, s0t)


def _lru_body(xb_ref, yb_ref, cbuf_ref, h0_ref, cw_ref, cb_ref, gw_ref, gb_ref, lam_ref,
              o_ref, ncbuf_ref, nh_ref, xbuf, hc):
    s, l, c = xb_ref.shape
    m = s * l
    nw = cw_ref.shape[0]
    t = pl.program_id(1)

    @pl.when(t == 0)
    def _():
        xbuf[:, HIST - (nw - 1):HIST, :] = cbuf_ref[...]
        hc[...] = h0_ref[...]

    @pl.when(t > 0)
    def _():
        xbuf[:, 0:HIST, :] = xbuf[:, l:l + HIST, :]

    xbuf[:, HIST:HIST + l, :] = xb_ref[...]
    xe = xbuf[...].reshape(s * (l + HIST), c)
    xc = cb_ref[...].reshape(1, 1, c)
    for j in range(nw):
        shift = nw - 1 - j
        xs = xe if shift == 0 else pltpu.roll(xe, shift, 0)
        xc = xc + xs.reshape(s, l + HIST, c)[:, HIST:, :] * cw_ref[j:j + 1, :].reshape(1, 1, c)
    xc = xc.reshape(m, c)
    gates = _dot(xc.astype(BF16), gw_ref[...]) + gb_ref[...]
    r = _sigmoid(gates[:, :c])
    i = _sigmoid(gates[:, c:])
    log_a = (-LRU_C * r) * _softplus(-lam_ref[...])
    a = jnp.exp(log_a)
    bx = jnp.sqrt(-_expm1(2.0 * log_a)) * i * xc
    a = a.reshape(m // SUBLANES, SUBLANES, c)
    bx = bx.reshape(m // SUBLANES, SUBLANES, c)
    pos = lax.broadcasted_iota(jnp.int32, a.shape, 1)
    d = 1
    while d < SUBLANES:
        valid = pos >= d
        a_prev = jnp.where(valid, pltpu.roll(a, d, 1), 1.0)
        b_prev = jnp.where(valid, pltpu.roll(bx, d, 1), 0.0)
        bx = a * b_prev + bx
        a = a * a_prev
        d *= 2
    a = a.reshape(s, l, c)
    bx = bx.reshape(s, l, c)
    h_prev = hc[...]
    groups = []
    for gi in range(l // SUBLANES):
        rows = slice(gi * SUBLANES, (gi + 1) * SUBLANES)
        hg = bx[:, rows, :] + a[:, rows, :] * h_prev
        groups.append(hg)
        h_prev = hg[:, SUBLANES - 1:SUBLANES, :]
    hs = jnp.concatenate(groups, axis=1).reshape(m, c)
    o_ref[...] = (hs * _gelu(yb_ref[...].reshape(m, c))).reshape(s, l, c).astype(o_ref.dtype)
    hc[...] = h_prev

    @pl.when(t == pl.num_programs(1) - 1)
    def _():
        ncbuf_ref[...] = xbuf[:, l + HIST - (nw - 1):l + HIST, :]
        nh_ref[...] = hc[...]


def _lru(xb, yb, cbuf, h0, cw, cb, gw, gb, lam, s, l):
    b, t, c = xb.shape
    nw = cw.shape[0]
    return pl.pallas_call(
        _lru_body,
        grid=(b // s, t // l),
        in_specs=[_tile_spec(s, l, c), _tile_spec(s, l, c), _seq_spec(s, nw - 1, c), _seq_spec(s, 1, c),
                  _const_spec(cw.shape), _const_spec((1, c)), _const_spec(gw.shape), _const_spec((1, 2 * c)),
                  _const_spec((1, c))],
        out_specs=[_tile_spec(s, l, c), _seq_spec(s, nw - 1, c), _seq_spec(s, 1, c)],
        out_shape=[jax.ShapeDtypeStruct((b, t, c), _act_dtype(l)), jax.ShapeDtypeStruct((b, nw - 1, c), F32),
                   jax.ShapeDtypeStruct((b, 1, c), F32)],
        scratch_shapes=[pltpu.VMEM((s, l + HIST, c), F32), pltpu.VMEM((s, 1, c), F32)],
        compiler_params=_params(("parallel", "arbitrary")),
        name="rg_lru",
    )(xb, yb, cbuf, h0, cw, cb, gw, gb, lam)


def _mix_ffn_body(*refs, n_in):
    a_refs, w_refs = refs[:n_in], refs[n_in:2 * n_in]
    x_ref, buf_ref, g1_ref, g2_ref, g3_ref, wup_ref, cw_ref, cb_ref, wdn_ref, y_ref, nbuf_ref, gbuf = refs[2 * n_in:]
    s, l, d = x_ref.shape
    m = s * l
    nw, f = cw_ref.shape
    t = pl.program_id(1)
    mix = None
    for a_ref, w_ref in zip(a_refs, w_refs):
        part = _dot(a_ref[...].reshape(m, a_ref.shape[-1]).astype(BF16), w_ref[...])
        mix = part if mix is None else mix + part
    x = x_ref[...].reshape(m, d) + _rms(mix, g1_ref[...])
    h = _rms(x, g2_ref[...]).astype(BF16)

    @pl.when(t == 0)
    def _():
        gbuf[:, HIST - (nw - 1):HIST, :] = buf_ref[...]

    @pl.when(t > 0)
    def _():
        gbuf[:, 0:HIST, :] = gbuf[:, l:l + HIST, :]

    gbuf[:, HIST:HIST + l, :] = _dot(h, wup_ref[:, :f]).reshape(s, l, f)
    u = _dot(h, wup_ref[:, f:])
    gc = cb_ref[...].reshape(1, 1, f)
    for j in range(nw):
        off = HIST - (nw - 1) + j
        gc = gc + gbuf[:, off:off + l, :] * cw_ref[j:j + 1, :].reshape(1, 1, f)
    act = (_gelu(gc).reshape(m, f) * u).astype(BF16)
    y = _dot(act, wdn_ref[...])
    y_ref[...] = (x + _rms(y, g3_ref[...])).reshape(s, l, d)

    @pl.when(t == pl.num_programs(1) - 1)
    def _():
        nbuf_ref[...] = gbuf[:, l + HIST - (nw - 1):l + HIST, :]


def _mix_ffn(acts, ws, x, buf, g1, g2, g3, wup, cw, cb, wdn, s, l):
    b, t, d = x.shape
    nw, f = cw.shape

    def resident(w):
        return pl.BlockSpec(w.shape, lambda b_, t_: (0, 0), pipeline_mode=pl.Buffered(1))

    return pl.pallas_call(
        functools.partial(_mix_ffn_body, n_in=len(acts)),
        grid=(b // s, t // l),
        in_specs=[_tile_spec(s, l, a.shape[-1]) for a in acts] + [resident(w) for w in ws]
                 + [_tile_spec(s, l, d), _seq_spec(s, nw - 1, f), _const_spec((1, d)), _const_spec((1, d)),
                    _const_spec((1, d)), resident(wup), _const_spec(cw.shape), _const_spec((1, f)), resident(wdn)],
        out_specs=[_tile_spec(s, l, d), _seq_spec(s, nw - 1, f)],
        out_shape=[jax.ShapeDtypeStruct((b, t, d), F32), jax.ShapeDtypeStruct((b, nw - 1, f), F32)],
        scratch_shapes=[pltpu.VMEM((s, l + HIST, f), F32)],
        compiler_params=_params(("parallel", "arbitrary")),
        name="mix_ffn",
    )(*acts, *ws, x, buf, g1, g2, g3, wup, cw, cb, wdn)


def _fox_prompt_body(qt_ref, ct_ref, kaug_ref, vaug_ref, o_ref, m_ref, acc_ref, sa_ref, sb_ref, cmax_a_ref, cmax_b_ref,
                     *, tq):
    hp = pl.program_id(1)
    iq = pl.program_id(2)
    qt = qt_ref[0]
    feat = lax.broadcasted_iota(jnp.int32, (LANES, 1), 0)
    qas = []
    for j in range(2):
        head = 2 * hp + j
        hi, mid, lo = _split3(ct_ref[0, pl.ds(head, 1), :])
        pick = jnp.where(((feat & (FOX_HEADS - 1)) == head) & (feat < 3 * FOX_HEADS), 1.0, 0.0)
        augt = jnp.where(feat == 3 * FOX_HEADS, hi,
                         jnp.where(feat == 3 * FOX_HEADS + 1, mid, jnp.where(feat == 3 * FOX_HEADS + 2, lo, pick)))
        own = (feat < FOX_HD) if j == 0 else (feat >= FOX_HD)
        qas.append(jnp.concatenate([jnp.where(own, qt, jnp.zeros_like(qt)), augt.astype(BF16)], axis=0))
    qa = jnp.concatenate(qas, axis=1)
    m_ref[...] = jnp.full_like(m_ref, NEG_INF)
    acc_ref[...] = jnp.zeros_like(acc_ref)
    key = lax.broadcasted_iota(jnp.int32, (tq, tq), 0)
    qry = lax.broadcasted_iota(jnp.int32, (tq, tq), 1)

    def logits(ik, buf):
        s_ref, cmax_ref = buf
        kt = kaug_ref[0, 0, pl.ds(pl.multiple_of(ik * tq, tq), tq), :]
        s2 = _dot(kt, qa)
        s_ref[...] = s2
        cmax_ref[...] = jnp.max(s2, axis=0, keepdims=True)

    def update(ik, buf, masked):
        s_ref, cmax_ref = buf
        for j in range(2):
            st = s_ref[:, j * tq:(j + 1) * tq]
            if masked:
                st = jnp.where(key <= qry, st, NEG_INF)
                tile_max = jnp.max(st, axis=0, keepdims=True)
            else:
                tile_max = cmax_ref[:, j * tq:(j + 1) * tq]
            m_old = m_ref[j]
            m_new = jnp.maximum(m_old, tile_max)
            pt = jnp.exp2(st - m_new).astype(BF16)
            m_ref[j] = m_new
            acc_ref[j] = acc_ref[j] * jnp.exp2(m_old - m_new) + _dot(vaug_ref[0, j, ik], pt)

    buf_a = (sa_ref, cmax_a_ref)
    buf_b = (sb_ref, cmax_b_ref)
    logits(0, buf_a)

    def pair_body(i, carry):
        logits(2 * i + 1, buf_b)
        update(2 * i, buf_a, False)
        logits(2 * i + 2, buf_a)
        update(2 * i + 1, buf_b, False)
        return carry

    lax.fori_loop(0, iq >> 1, pair_body, 0)

    @pl.when((iq & 1) == 0)
    def _():
        update(iq, buf_a, True)

    @pl.when((iq & 1) == 1)
    def _():
        logits(iq, buf_b)
        update(iq - 1, buf_a, False)
        update(iq, buf_b, True)

    a0 = acc_ref[0]
    a1 = acc_ref[1]
    ot = jnp.concatenate([a0[:FOX_HD] / a0[FOX_HD:], a1[FOX_HD:] / a1[:FOX_HD]], axis=0)
    o_ref[0] = ot.T.astype(o_ref.dtype)


def _fox_prompt(qt, ct2, kaug, vaug, tq):
    b, da, t = qt.shape
    nq = t // tq
    assert vaug.shape[2] == nq and vaug.shape[-1] == tq
    return pl.pallas_call(
        functools.partial(_fox_prompt_body, tq=tq),
        grid=(b, FOX_HEADS // 2, nq),
        in_specs=[pl.BlockSpec((1, LANES, tq), lambda b_, h_, i_: (b_, h_, i_)),
                  pl.BlockSpec((1, FOX_HEADS, tq), lambda b_, h_, i_: (b_, 0, i_)),
                  pl.BlockSpec((1, 1) + kaug.shape[2:], lambda b_, h_, i_: (b_, h_, 0, 0)),
                  pl.BlockSpec((1, 2) + vaug.shape[2:], lambda b_, h_, i_: (b_, h_, 0, 0, 0))],
        out_specs=pl.BlockSpec((1, tq, LANES), lambda b_, h_, i_: (b_, i_, h_)),
        out_shape=jax.ShapeDtypeStruct((b, t, da), BF16),
        scratch_shapes=[pltpu.VMEM((2, 1, tq), F32), pltpu.VMEM((2, LANES, tq), F32),
                        pltpu.VMEM((tq, 2 * tq), F32), pltpu.VMEM((tq, 2 * tq), F32),
                        pltpu.VMEM((1, 2 * tq), F32), pltpu.VMEM((1, 2 * tq), F32)],
        compiler_params=_params(("parallel", "parallel", "arbitrary")),
        name="fox_prompt",
    )(qt, ct2, kaug, vaug)


def _fox_sample_body(pt_ref, q_ref, kn_ref, vn_ref, cnt_ref, cnr_ref, *rest, npp):
    kp = rest[0:npp]
    vp = rest[npp:2 * npp]
    lp = rest[2 * npp:3 * npp]
    o_ref, m_ref, l_ref, acc_ref, carry_ref, kb_ref, vb_ref = rest[3 * npp:]
    del pt_ref
    j = pl.program_id(1)
    t, da = q_ref.shape[1], q_ref.shape[2]
    rows = FOX_HEADS * t
    row_c = lax.broadcasted_iota(jnp.int32, (rows, da), 0)
    col_c = lax.broadcasted_iota(jnp.int32, (rows, da), 1)
    own = (col_c & -FOX_HD) == (row_c & -t) * (FOX_HD // t)
    q8 = q_ref[0] * FOX_SCALE
    qbd = jnp.where(own, jnp.concatenate([q8] * FOX_HEADS, axis=0), 0.0).astype(BF16)
    cn_rows = cnr_ref[0]

    def head_rows(x):
        return jnp.concatenate([jnp.broadcast_to(x[h:h + 1, :], (t, x.shape[1])) for h in range(FOX_HEADS)], axis=0)

    @pl.when(j == 0)
    def _():
        pad = jnp.zeros((PAGE_SIZE - t, da), F32)
        kn = jnp.concatenate([kn_ref[0], pad], axis=0).astype(BF16)
        vn = jnp.concatenate([vn_ref[0], pad], axis=0).astype(BF16)
        row_l = lax.broadcasted_iota(jnp.int32, (rows, PAGE_SIZE), 0)
        col_l = lax.broadcasted_iota(jnp.int32, (rows, PAGE_SIZE), 1)
        sc = _dot_nt(qbd, kn) + cn_rows - head_rows(cnt_ref[0])
        sc = jnp.where(col_l <= (row_l & (t - 1)), sc, NEG_INF)
        m0 = jnp.max(sc, axis=-1, keepdims=True)
        p = jnp.exp(sc - m0)
        m_ref[...] = m0
        l_ref[...] = jnp.sum(p, axis=-1, keepdims=True)
        acc_ref[...] = _dot(p.astype(BF16), vn)
        carry_ref[...] = jnp.zeros_like(carry_ref)

    prow = lax.broadcasted_iota(jnp.int32, (PAGE_SIZE, PAGE_SIZE), 0)
    pcol = lax.broadcasted_iota(jnp.int32, (PAGE_SIZE, PAGE_SIZE), 1)
    later = jnp.where(prow > pcol, 1.0, 0.0).astype(F32)
    for i in range(npp):
        kb_ref[:, i * PAGE_SIZE:(i + 1) * PAGE_SIZE] = kp[i][...].astype(BF16)
        vb_ref[:, i * PAGE_SIZE:(i + 1) * PAGE_SIZE] = vp[i][...].astype(BF16)
    lf_all = jnp.concatenate([lp[i][...] for i in range(npp)], axis=0)
    inside = _dot(lf_all, later, HIGHEST)
    biases = []
    carry = carry_ref[...]
    for i in range(npp):
        rows_i = slice(i * FOX_HEADS, (i + 1) * FOX_HEADS)
        biases.append(head_rows(inside[rows_i] + carry))
        carry = carry + (inside[rows_i, 0:1] + lf_all[rows_i, 0:1])
    carry_ref[...] = carry
    sc = _dot(qbd, kb_ref[...]) + cn_rows + jnp.concatenate(biases, axis=1)
    m_old = m_ref[...]
    m_new = jnp.maximum(m_old, jnp.max(sc, axis=-1, keepdims=True))
    alpha = jnp.exp(m_old - m_new)
    p = jnp.exp(sc - m_new)
    l_new = alpha * l_ref[...] + jnp.sum(p, axis=-1, keepdims=True)
    acc = acc_ref[...] * alpha + _dot_nt(p.astype(BF16), vb_ref[...])
    m_ref[...] = m_new
    l_ref[...] = l_new
    acc_ref[...] = acc

    @pl.when(j == pl.num_programs(1) - 1)
    def _():
        full = jnp.where(own, acc / l_new, 0.0)
        out = full[0:t, :]
        for h in range(1, FOX_HEADS):
            out = out + full[h * t:(h + 1) * t, :]
        o_ref[0] = out


def _fox_sample(q, k, v, c, cache_k, cache_v, cache_logf, page_table, layer):
    b, t, da = q.shape
    n_pages = page_table.shape[1]
    npp = PAGES_PER_STEP
    assert n_pages % npp == 0 and PAGE_SIZE % t == 0 and t % SUBLANES == 0
    n_pool, n_layers = cache_k.shape[0], cache_k.shape[1]
    ckt = cache_k.transpose(0, 1, 3, 4, 2).reshape(n_pool, n_layers, da, PAGE_SIZE)
    cvt = cache_v.transpose(0, 1, 3, 4, 2).reshape(n_pool, n_layers, da, PAGE_SIZE)
    clt = cache_logf.transpose(0, 1, 3, 2)
    ct = c.transpose(0, 2, 1)
    cn_t = jnp.pad(ct, ((0, 0), (0, 0), (0, PAGE_SIZE - t)))
    cn_rows = ct.reshape(b, FOX_HEADS * t, 1)

    def page_spec(i, height):
        return pl.BlockSpec((None, None, height, PAGE_SIZE),
                            lambda b_, j_, pt: (pt[b_, n_pages - 1 - (j_ * npp + i)], layer, 0, 0))

    tok = pl.BlockSpec((1, t, da), lambda b_, j_, pt: (b_, 0, 0))
    rows = FOX_HEADS * t
    grid_spec = pltpu.PrefetchScalarGridSpec(
        num_scalar_prefetch=1,
        grid=(b, n_pages // npp),
        in_specs=[tok, tok, tok,
                  pl.BlockSpec((1, FOX_HEADS, PAGE_SIZE), lambda b_, j_, pt: (b_, 0, 0)),
                  pl.BlockSpec((1, rows, 1), lambda b_, j_, pt: (b_, 0, 0))]
                 + [page_spec(i, da) for i in range(npp)] * 2
                 + [page_spec(i, FOX_HEADS) for i in range(npp)],
        out_specs=tok,
        scratch_shapes=[pltpu.VMEM((rows, 1), F32), pltpu.VMEM((rows, 1), F32), pltpu.VMEM((rows, da), F32),
                        pltpu.VMEM((FOX_HEADS, 1), F32),
                        pltpu.VMEM((da, npp * PAGE_SIZE), BF16), pltpu.VMEM((da, npp * PAGE_SIZE), BF16)],
    )
    return pl.pallas_call(
        functools.partial(_fox_sample_body, npp=npp),
        grid_spec=grid_spec,
        out_shape=jax.ShapeDtypeStruct((b, t, da), F32),
        compiler_params=_params(("parallel", "arbitrary")),
        name="fox_sample",
    )(page_table, q, k, v, cn_t, cn_rows, *([ckt] * npp), *([cvt] * npp), *([clt] * npp))


def _pow2_tile(t, want):
    l = min(t, want)
    assert t % l == 0 and l & (l - 1) == 0
    return l


def _prep_weights(p):
    w = {}
    n_a = p['w_in_a'].shape[0]
    wa = p['w_in_a']
    d = wa.shape[1]
    q_end = 2 * GLA_HEADS * GLA_DK + GLA_HEADS * GLA_DV
    rank = p['w_gk2'].shape[1]
    w['w_in_a'] = jnp.concatenate(
        [wa[:, :, :q_end], wa[:, :, q_end + rank:], wa[:, :, q_end:q_end + rank],
         jnp.zeros((n_a, d, LANES - rank), wa.dtype)], axis=-1).astype(BF16)
    w['w_gk2'] = jnp.pad(p['w_gk2'], ((0, 0), (0, LANES - rank), (0, 0)))
    gw = p['lru_gate_w']
    nb, bw = gw.shape[2], gw.shape[3]
    eye = jnp.eye(nb, dtype=gw.dtype)
    dense = jnp.einsum('agncd,nm->agncmd', gw, eye).reshape(n_a, 2, nb * bw, nb * bw)
    w['lru_gate_w'] = jnp.concatenate([dense[:, 0], dense[:, 1]], axis=-1).astype(BF16)
    w['lru_gate_b'] = p['lru_gate_b'].reshape(n_a, 1, -1)
    w['w_out_a'] = p['w_out_a'].astype(BF16)
    wc = p['w_in_c']
    n_c = wc.shape[0]
    da = FOX_HEADS * FOX_HD
    w['w_in_c'] = jnp.concatenate([wc, jnp.zeros((n_c, d, LANES - FOX_HEADS), wc.dtype)], axis=-1).astype(BF16)
    w['b_f'] = jnp.pad(p['b_f'], ((0, 0), (0, LANES - FOX_HEADS))).reshape(n_c, 1, LANES)
    w['w_qkvf_t'] = jnp.swapaxes(w['w_in_c'], 1, 2)
    w['b_f_col'] = w['b_f'].reshape(n_c, LANES, 1)
    w['eye'] = jnp.eye(LANES, dtype=BF16)
    w['w_out_c'] = p['w_out_c'].astype(BF16)
    w['ffn_w_up'] = p['ffn_w_up'].astype(BF16)
    w['ffn_w_down'] = p['ffn_w_down'].astype(BF16)
    return w


def _trunk(x, gla_s, lru_buf, lru_h, ffn_buf, layer_c, p, w, s, tiles):
    b, t, d = x.shape
    depth = p['norm_g'].shape[0]
    l_proj, l_gla, l_lru, l_ffn = (_pow2_tile(t, n) for n in tiles)
    chunk = np.gcd(t, GLA_CHUNK).item()
    a_widths = [GLA_HEADS * GLA_DK] * 2 + [GLA_HEADS * GLA_DV] * 2 + [lru_buf.shape[-1]] * 2 + [LANES]
    w_split = GLA_HEADS * GLA_DV
    gla_l, buf_l, h_l, k_l, v_l, lf_l, ffn_l = [], [], [], [], [], [], []
    ia = ic = 0
    for layer in range(depth):
        ng = p['norm_g'][layer].reshape(4, 1, d)
        if layer % 2 == 0:
            q, k, v, gg, xb, yb, lo = _norm_proj(x, ng[0], w['w_in_a'][ia], a_widths, s, l_proj)
            s0t = gla_s[:, ia].reshape(b, GLA_HEADS * GLA_DK, GLA_DV).transpose(0, 2, 1)
            o, snt = _gla(q, k, v, lo, gg, w['w_gk2'][ia], p['b_gk2'][ia].reshape(1, -1),
                          p['gla_norm_g'][ia].reshape(1, -1), s0t, s if s > 1 else min(b, GLA_SEQS), l_gla, chunk)
            lru_o, nbuf, nh = _lru(xb, yb, lru_buf[:, ia], lru_h[:, ia][:, None, :], p['lru_conv_w'][ia],
                                   p['lru_conv_b'][ia].reshape(1, -1), w['lru_gate_w'][ia], w['lru_gate_b'][ia],
                                   p['lru_lambda'][ia].reshape(1, -1), s, l_lru)
            acts, ws = [o, lru_o], [w['w_out_a'][ia][:w_split], w['w_out_a'][ia][w_split:]]
            gla_l.append(snt.transpose(0, 2, 1).reshape(b, GLA_HEADS, GLA_DK, GLA_DV))
            buf_l.append(nbuf)
            h_l.append(nh[:, 0, :])
            ia += 1
        else:
            o, k_new, v_new, lf_new = layer_c(x, ng[0], ic)
            acts, ws = [o], [w['w_out_c'][ic]]
            k_l.append(k_new)
            v_l.append(v_new)
            lf_l.append(lf_new)
            ic += 1
        x, fb = _mix_ffn(acts, ws, x, ffn_buf[:, layer], ng[1], ng[2], ng[3], w['ffn_w_up'][layer],
                         p['ffn_conv_w'][layer], p['ffn_conv_b'][layer].reshape(1, -1), w['ffn_w_down'][layer], s, l_ffn)
        ffn_l.append(fb)
    return (x, jnp.stack(gla_l, axis=1), jnp.stack(buf_l, axis=1), jnp.stack(h_l, axis=1), jnp.stack(k_l, axis=1),
            jnp.stack(v_l, axis=1), jnp.stack(lf_l, axis=1), jnp.stack(ffn_l, axis=1))


PROMPT_TILES = (512, 512, 256, 512)
FOX_TQ = 512


def kernel(x_prompt, x_sample, state_gla, state_lru_conv, state_lru_h, cache_k, cache_v, cache_logf, state_ffn_conv, page_table, norm_g, w_in_a, w_gk2, b_gk2, gla_norm_g, lru_conv_w, lru_conv_b, lru_gate_w, lru_gate_b, lru_lambda, w_out_a, w_in_c, b_f, w_out_c, ffn_w_up, ffn_conv_w, ffn_conv_b, ffn_w_down):
    p = {'norm_g': norm_g, 'w_in_a': w_in_a, 'w_gk2': w_gk2, 'b_gk2': b_gk2, 'gla_norm_g': gla_norm_g,
         'lru_conv_w': lru_conv_w, 'lru_conv_b': lru_conv_b, 'lru_gate_w': lru_gate_w, 'lru_gate_b': lru_gate_b,
         'lru_lambda': lru_lambda, 'w_out_a': w_out_a, 'w_in_c': w_in_c, 'b_f': b_f, 'w_out_c': w_out_c,
         'ffn_w_up': ffn_w_up, 'ffn_conv_w': ffn_conv_w, 'ffn_conv_b': ffn_conv_b, 'ffn_w_down': ffn_w_down}
    w = _prep_weights(p)
    bp = x_prompt.shape[0]
    bs, ts, _ = x_sample.shape
    dt = x_prompt.dtype
    gla0 = jnp.zeros((bp,) + state_gla.shape[1:], dt)
    lru_buf0 = jnp.zeros((bp,) + state_lru_conv.shape[1:], dt)
    lru_h0 = jnp.zeros((bp,) + state_lru_h.shape[1:], dt)
    ffn_buf0 = jnp.zeros((bp,) + state_ffn_conv.shape[1:], dt)

    def layer_c_prompt(x, g, ic):
        b, t, _ = x.shape
        l = _pow2_tile(t, FOX_TQ)
        qt, kt, vt, lft, ct2, kaug, vaug = _proj_c_prompt(x, g, w['w_qkvf_t'][ic], w['b_f_col'][ic], w['eye'], l)
        o = _fox_prompt(qt, ct2, kaug, vaug, l)
        k_new = kt.reshape(b, FOX_HEADS, FOX_HD, t).transpose(0, 3, 1, 2)
        v_new = vt.reshape(b, FOX_HEADS, FOX_HD, t).transpose(0, 3, 1, 2)
        return o, k_new, v_new, lft.transpose(0, 2, 1)

    def layer_c_sample(x, g, ic):
        b, t, _ = x.shape
        q, k, v, lf, c = _proj_c_sample(x, g, w['w_in_c'][ic], w['b_f'][ic])
        o = _fox_sample(q, k, v, c, cache_k, cache_v, cache_logf, page_table, ic)
        return o, k.reshape(b, t, FOX_HEADS, FOX_HD), v.reshape(b, t, FOX_HEADS, FOX_HD), lf

    outs_p = _trunk(x_prompt, gla0, lru_buf0, lru_h0, ffn_buf0, layer_c_prompt, p, w, 1, PROMPT_TILES)
    outs_s = _trunk(x_sample, state_gla, state_lru_conv, state_lru_h, state_ffn_conv, layer_c_sample, p, w, bs,
                    (ts, ts, ts, ts))
    return tuple(o for pair in zip(outs_p, outs_s) for o in pair)
```

```python
import functools

import numpy as np
import jax
import jax.numpy as jnp
from jax import lax
from jax.experimental import pallas as pl
from jax.experimental.pallas import tpu as pltpu

F32 = jnp.float32
BF16 = jnp.bfloat16
HIGHEST = lax.Precision.HIGHEST

EPS = 1e-6
GLA_HEADS = 4
GLA_DK = 64
GLA_DV = 128
GLA_GATE_NORM = 16.0
GLA_CHUNK = 64
LRU_C = 8.0
FOX_HEADS = 16
FOX_HD = 64
FOX_SCALE = FOX_HD ** -0.5
PAGE_SIZE = 128
LANES = 128
SUBLANES = 8
HIST = SUBLANES
VMEM_LIMIT = 56 * 2 ** 20
PAGES_PER_STEP = 16
GLA_SEQS = 2
GLA_CHUNKS = 4
NEG_INF = float("-inf")
LOG2E = 1.4426950408889634


def _rms(x, g):
    return x * lax.rsqrt(jnp.mean(x * x, axis=-1, keepdims=True) + EPS) * g


def _sigmoid(x):
    return 0.5 * (1.0 + jnp.tanh(0.5 * x))


def _softplus(x):
    return jnp.maximum(x, 0.0) + jnp.log1p(jnp.exp(-jnp.abs(x)))


def _log_sigmoid(x):
    return -_softplus(-x)


def _expm1(x):
    return jnp.tanh(0.5 * x) * (jnp.exp(x) + 1.0)


def _gelu(x):
    return x * (0.5 * (1.0 + jnp.tanh(0.7978845608028654 * (x + 0.044715 * (x * x * x)))))


def _dot(a, b, precision=None):
    return jnp.dot(a, b, preferred_element_type=F32, precision=precision)


def _dot_nt(a, b, precision=None):
    return lax.dot_general(a, b, (((1,), (1,)), ((), ())), preferred_element_type=F32, precision=precision)


def _dot_tn(a, b, precision=None):
    return lax.dot_general(a, b, (((0,), (0,)), ((), ())), preferred_element_type=F32, precision=precision)


def _seq_tril(m, l):
    row = lax.broadcasted_iota(jnp.int32, (m, m), 0)
    col = lax.broadcasted_iota(jnp.int32, (m, m), 1)
    same = (row & -l) == (col & -l)
    return jnp.where(same & (col <= row), 1.0, 0.0).astype(F32)


def _split3(x):
    hi = x.astype(BF16).astype(F32)
    mid = (x - hi).astype(BF16).astype(F32)
    lo = (x - hi - mid).astype(BF16).astype(F32)
    return hi, mid, lo


def _act_dtype(rows):
    return BF16 if rows % (2 * SUBLANES) == 0 else F32


def _params(sem):
    return pltpu.CompilerParams(dimension_semantics=sem, vmem_limit_bytes=VMEM_LIMIT)


def _tile_spec(s, l, n):
    return pl.BlockSpec((s, l, n), lambda b, t: (b, t, 0))


def _seq_spec(s, r, n):
    return pl.BlockSpec((s, r, n), lambda b, t: (b, 0, 0))


def _const_spec(shape):
    return pl.BlockSpec(shape, lambda b, t: (0,) * len(shape))


def _norm_proj_body(x_ref, g_ref, w_ref, *o_refs):
    s, l, d = x_ref.shape
    h = _rms(x_ref[...].reshape(s * l, d), g_ref[...]).astype(BF16)
    off = 0
    for o_ref in o_refs:
        n = o_ref.shape[-1]
        o_ref[...] = _dot(h, w_ref[:, off:off + n]).reshape(s, l, n).astype(o_ref.dtype)
        off += n


def _norm_proj(x, g, w, widths, s, l):
    b, t, d = x.shape
    return pl.pallas_call(
        _norm_proj_body,
        grid=(b // s, t // l),
        in_specs=[_tile_spec(s, l, d), _const_spec((1, d)), _const_spec(w.shape)],
        out_specs=[_tile_spec(s, l, n) for n in widths],
        out_shape=[jax.ShapeDtypeStruct((b, t, n), F32) for n in widths],
        compiler_params=_params(("parallel", "parallel")),
        name="norm_proj_a",
    )(x, g, w)


def _proj_cs_body(x_ref, g_ref, w_ref, bf_ref, q_ref, k_ref, v_ref, lf_ref, c_ref):
    s, l, d = x_ref.shape
    m = s * l
    da = q_ref.shape[-1]
    nh = lf_ref.shape[-1]
    h = _rms(x_ref[...].reshape(m, d), g_ref[...]).astype(BF16)
    q_ref[...] = _dot(h, w_ref[:, 0:da]).reshape(s, l, da)
    k_ref[...] = _dot(h, w_ref[:, da:2 * da]).reshape(s, l, da)
    v_ref[...] = _dot(h, w_ref[:, 2 * da:3 * da]).reshape(s, l, da)
    lf = _log_sigmoid(_dot(h, w_ref[:, 3 * da:]) + bf_ref[...])
    c = _dot(_seq_tril(m, l), lf, HIGHEST)
    lf_ref[...] = lf[:, :nh].reshape(s, l, nh)
    c_ref[...] = c[:, :nh].reshape(s, l, nh)


def _proj_c_sample(x, g, w, bf):
    b, t, d = x.shape
    da = FOX_HEADS * FOX_HD
    wide = jax.ShapeDtypeStruct((b, t, da), F32)
    narrow = jax.ShapeDtypeStruct((b, t, FOX_HEADS), F32)
    return pl.pallas_call(
        _proj_cs_body,
        grid=(1, 1),
        in_specs=[_tile_spec(b, t, d), _const_spec((1, d)), _const_spec(w.shape), _const_spec((1, LANES))],
        out_specs=[_tile_spec(b, t, da)] * 3 + [_tile_spec(b, t, FOX_HEADS)] * 2,
        out_shape=[wide, wide, wide, narrow, narrow],
        compiler_params=_params(("parallel", "arbitrary")),
        name="norm_proj_c_sample",
    )(x, g, w, bf)


def _proj_cp_body(x_ref, g_ref, wt_ref, bfc_ref, eye_ref,
                  qt_ref, kt_ref, vt_ref, lft_ref, ct2_ref, kaug_ref, vaug_ref, carry_ref):
    _, l, d = x_ref.shape
    da = kt_ref.shape[1]
    h = _rms(x_ref[0], g_ref[...]).astype(BF16)
    allt = _dot_nt(wt_ref[...], h)
    qt_ref[0] = (allt[0:da] * (FOX_SCALE * LOG2E)).astype(qt_ref.dtype)
    kt = allt[da:2 * da]
    kt_ref[0] = kt
    vt = allt[2 * da:3 * da]
    vt_ref[0] = vt
    lft = _log_sigmoid(allt[3 * da:] + bfc_ref[...])
    lft_ref[0] = lft[:FOX_HEADS]

    @pl.when(pl.program_id(1) == 0)
    def _():
        carry_ref[...] = jnp.zeros_like(carry_ref)

    row = lax.broadcasted_iota(jnp.int32, (l, l), 0)
    col = lax.broadcasted_iota(jnp.int32, (l, l), 1)
    ct = _dot(lft, jnp.where(row <= col, 1.0, 0.0).astype(F32), HIGHEST) + carry_ref[...]
    carry_ref[...] = ct[:, l - 1:l]
    ct2 = ct[:FOX_HEADS] * LOG2E
    ct2_ref[0] = ct2
    hi, mid, lo = _split3(ct2)
    ones_rows = jnp.where(lax.broadcasted_iota(jnp.int32, (FOX_HEADS, l), 0) < 3, 1.0, 0.0)
    augt = jnp.concatenate([-hi, -mid, -lo, ones_rows, jnp.zeros((LANES - 4 * FOX_HEADS, l), F32)], axis=0).astype(BF16)
    aug = _dot_tn(augt, eye_ref[...]).astype(BF16)
    pair = 2 * FOX_HD
    for p in range(FOX_HEADS // 2):
        kaug_ref[0, p] = jnp.concatenate([kt[p * pair:(p + 1) * pair].T.astype(BF16), aug], axis=1)
    vtb = vt.astype(BF16)
    ones = jnp.ones((FOX_HD, l), BF16)
    for hh in range(FOX_HEADS):
        vh = vtb[hh * FOX_HD:(hh + 1) * FOX_HD]
        vaug_ref[0, hh, 0] = jnp.concatenate([vh, ones] if hh % 2 == 0 else [ones, vh], axis=0)


def _proj_c_prompt(x, g, wt, bfc, eye, l):
    b, t, d = x.shape
    da = FOX_HEADS * FOX_HD
    nk = t // l
    pair = 2 * FOX_HD
    feat_t = pl.BlockSpec((1, da, l), lambda b_, t_: (b_, 0, t_))
    head_t = pl.BlockSpec((1, FOX_HEADS, l), lambda b_, t_: (b_, 0, t_))
    return pl.pallas_call(
        _proj_cp_body,
        grid=(b, nk),
        in_specs=[_tile_spec(1, l, d), _const_spec((1, d)), _const_spec(wt.shape),
                  _const_spec((LANES, 1)), _const_spec((LANES, LANES))],
        out_specs=[feat_t, feat_t, feat_t, head_t, head_t,
                   pl.BlockSpec((1, FOX_HEADS // 2, l, pair + LANES), lambda b_, t_: (b_, 0, t_, 0)),
                   pl.BlockSpec((1, FOX_HEADS, 1, pair, l), lambda b_, t_: (b_, 0, t_, 0, 0))],
        out_shape=[jax.ShapeDtypeStruct((b, da, t), BF16),
                   jax.ShapeDtypeStruct((b, da, t), F32),
                   jax.ShapeDtypeStruct((b, da, t), F32),
                   jax.ShapeDtypeStruct((b, FOX_HEADS, t), F32),
                   jax.ShapeDtypeStruct((b, FOX_HEADS, t), F32),
                   jax.ShapeDtypeStruct((b, FOX_HEADS // 2, t, pair + LANES), BF16),
                   jax.ShapeDtypeStruct((b, FOX_HEADS, nk, pair, l), BF16)],
        scratch_shapes=[pltpu.VMEM((LANES, 1), F32)],
        compiler_params=_params(("parallel", "arbitrary")),
        name="norm_proj_c_prompt",
    )(x, g, wt, bfc, eye)


def _gla_body(q_ref, k_ref, v_ref, lo_ref, gg_ref, wgk_ref, bgk_ref, ng_ref, s0_ref, o_ref, sn_ref, st_ref, *, chunk):
    s, l, dq = q_ref.shape
    nc = l // chunk
    mm = BF16 if chunk >= 16 else F32
    t = pl.program_id(1)

    @pl.when(t == 0)
    def _():
        st_ref[...] = s0_ref[...]

    tri = _seq_tril(chunk, chunk)
    lane = lax.broadcasted_iota(jnp.int32, (1, dq), 1)
    wgk = wgk_ref[...].astype(mm)
    bgk = bgk_ref[...]
    ng = ng_ref[...]

    heads = range(GLA_HEADS)
    hms = [(lane >= h * GLA_DK) & (lane < (h + 1) * GLA_DK) for h in heads]

    def do_chunks(sis, row_slices):
        jobs = [(si, rows) for si in sis for rows in row_slices]
        la = [_log_sigmoid(_dot(lo_ref[si, rows, :].astype(mm), wgk) + bgk) / GLA_GATE_NORM for si, rows in jobs]
        bcum = [_dot(tri, x, HIGHEST) for x in la]
        blast = [x[chunk - 1:chunk, :] for x in bcum]
        qm, kd, kd2, vh = [], [], [], []
        for n, (si, rows) in enumerate(jobs):
            ki = k_ref[si, rows, :]
            vi = v_ref[si, rows, :]
            qd = q_ref[si, rows, :] * (GLA_DK ** -0.5) * jnp.exp(bcum[n])
            qm.append([jnp.where(hms[h], qd, 0.0).astype(mm) for h in heads])
            kd.append((ki * jnp.exp(-bcum[n])).astype(mm))
            kd2.append((ki * jnp.exp(blast[n] - bcum[n])).astype(mm))
            vh.append([vi[:, h * GLA_DV:(h + 1) * GLA_DV].astype(mm) for h in heads])
        nj = range(len(jobs))
        att = [[_dot_nt(qm[n][h], kd[n]) for h in heads] for n in nj]
        upd = [[_dot_tn(vh[n][h], kd2[n]) for h in heads] for n in nj]
        att = [[jnp.where(tri > 0.0, att[n][h], 0.0).astype(mm) for h in heads] for n in nj]
        intra = [[_dot(att[n][h], vh[n][h]) for h in heads] for n in nj]
        st = [st_ref[si] for si in sis]
        inter = [None] * len(jobs)
        for c in range(len(row_slices)):
            ns = [u * len(row_slices) + c for u in range(len(sis))]
            for u, n in enumerate(ns):
                stm = st[u].astype(mm)
                inter[n] = [_dot_nt(qm[n][h], stm) for h in heads]
            for u, n in enumerate(ns):
                new = st[u] * jnp.exp(blast[n])
                for h in heads:
                    new = new + jnp.where(hms[h], upd[n][h], 0.0)
                st[u] = new
        for u, si in enumerate(sis):
            st_ref[si] = st[u]
        for n, (si, rows) in enumerate(jobs):
            gi = gg_ref[si, rows, :]
            for h in heads:
                gh = gi[:, h * GLA_DV:(h + 1) * GLA_DV]
                o = intra[n][h] + inter[n][h]
                o_ref[si, rows, h * GLA_DV:(h + 1) * GLA_DV] = (_rms(o, ng) * (gh * _sigmoid(gh))).astype(o_ref.dtype)

    group = min(s, GLA_SEQS)
    step = min(nc, GLA_CHUNKS)

    def do_group(gi, carry):
        sis = [gi * group + u for u in range(group)]
        if nc == step:
            do_chunks(sis, [slice(c * chunk, (c + 1) * chunk) for c in range(nc)])
        else:
            def body(ci, c2):
                do_chunks(sis, [pl.ds(pl.multiple_of((ci * step + c) * chunk, chunk), chunk) for c in range(step)])
                return c2
            lax.fori_loop(0, nc // step, body, 0)
        return carry

    if s == group:
        do_group(0, 0)
    else:
        lax.fori_loop(0, s // group, do_group, 0)

    @pl.when(t == pl.num_programs(1) - 1)
    def _():
        sn_ref[...] = st_ref[...]


def _gla(q, k, v, lo, gg, wgk, bgk, ng, s0t, s, l, chunk):
    b, t, dq = q.shape
    dv = v.shape[-1]
    return pl.pallas_call(
        functools.partial(_gla_body, chunk=chunk),
        grid=(b // s, t // l),
        in_specs=[_tile_spec(s, l, dq), _tile_spec(s, l, dq), _tile_spec(s, l, dv), _tile_spec(s, l, LANES),
                  _tile_spec(s, l, dv), _const_spec(wgk.shape), _const_spec((1, dq)), _const_spec((1, GLA_DV)),
                  _seq_spec(s, GLA_DV, dq)],
        out_specs=[_tile_spec(s, l, dv), _seq_spec(s, GLA_DV, dq)],
        out_shape=[jax.ShapeDtypeStruct((b, t, dv), _act_dtype(chunk)), jax.ShapeDtypeStruct((b, GLA_DV, dq), F32)],
        scratch_shapes=[pltpu.VMEM((s, GLA_DV, dq), F32)],
        compiler_params=_params(("parallel", "arbitrary")),
        name="gla",
    )(q, k, v, lo, gg, wgk, bgk, ng, s0t)


def _lru_body(xb_ref, yb_ref, cbuf_ref, h0_ref, cw_ref, cb_ref, gw_ref, gb_ref, lam_ref,
              o_ref, ncbuf_ref, nh_ref, xbuf, hc):
    s, l, c = xb_ref.shape
    m = s * l
    nw = cw_ref.shape[0]
    t = pl.program_id(1)

    @pl.when(t == 0)
    def _():
        xbuf[:, HIST - (nw - 1):HIST, :] = cbuf_ref[...]
        hc[...] = h0_ref[...]

    @pl.when(t > 0)
    def _():
        xbuf[:, 0:HIST, :] = xbuf[:, l:l + HIST, :]

    xbuf[:, HIST:HIST + l, :] = xb_ref[...]
    xe = xbuf[...].reshape(s * (l + HIST), c)
    xc = cb_ref[...].reshape(1, 1, c)
    for j in range(nw):
        shift = nw - 1 - j
        xs = xe if shift == 0 else pltpu.roll(xe, shift, 0)
        xc = xc + xs.reshape(s, l + HIST, c)[:, HIST:, :] * cw_ref[j:j + 1, :].reshape(1, 1, c)
    xc = xc.reshape(m, c)
    gates = _dot(xc.astype(BF16), gw_ref[...]) + gb_ref[...]
    r = _sigmoid(gates[:, :c])
    i = _sigmoid(gates[:, c:])
    log_a = (-LRU_C * r) * _softplus(-lam_ref[...])
    a = jnp.exp(log_a)
    bx = jnp.sqrt(-_expm1(2.0 * log_a)) * i * xc
    a = a.reshape(m // SUBLANES, SUBLANES, c)
    bx = bx.reshape(m // SUBLANES, SUBLANES, c)
    pos = lax.broadcasted_iota(jnp.int32, a.shape, 1)
    d = 1
    while d < SUBLANES:
        valid = pos >= d
        a_prev = jnp.where(valid, pltpu.roll(a, d, 1), 1.0)
        b_prev = jnp.where(valid, pltpu.roll(bx, d, 1), 0.0)
        bx = a * b_prev + bx
        a = a * a_prev
        d *= 2
    a = a.reshape(s, l, c)
    bx = bx.reshape(s, l, c)
    h_prev = hc[...]
    groups = []
    for gi in range(l // SUBLANES):
        rows = slice(gi * SUBLANES, (gi + 1) * SUBLANES)
        hg = bx[:, rows, :] + a[:, rows, :] * h_prev
        groups.append(hg)
        h_prev = hg[:, SUBLANES - 1:SUBLANES, :]
    hs = jnp.concatenate(groups, axis=1).reshape(m, c)
    o_ref[...] = (hs * _gelu(yb_ref[...].reshape(m, c))).reshape(s, l, c).astype(o_ref.dtype)
    hc[...] = h_prev

    @pl.when(t == pl.num_programs(1) - 1)
    def _():
        ncbuf_ref[...] = xbuf[:, l + HIST - (nw - 1):l + HIST, :]
        nh_ref[...] = hc[...]


def _lru(xb, yb, cbuf, h0, cw, cb, gw, gb, lam, s, l):
    b, t, c = xb.shape
    nw = cw.shape[0]
    return pl.pallas_call(
        _lru_body,
        grid=(b // s, t // l),
        in_specs=[_tile_spec(s, l, c), _tile_spec(s, l, c), _seq_spec(s, nw - 1, c), _seq_spec(s, 1, c),
                  _const_spec(cw.shape), _const_spec((1, c)), _const_spec(gw.shape), _const_spec((1, 2 * c)),
                  _const_spec((1, c))],
        out_specs=[_tile_spec(s, l, c), _seq_spec(s, nw - 1, c), _seq_spec(s, 1, c)],
        out_shape=[jax.ShapeDtypeStruct((b, t, c), _act_dtype(l)), jax.ShapeDtypeStruct((b, nw - 1, c), F32),
                   jax.ShapeDtypeStruct((b, 1, c), F32)],
        scratch_shapes=[pltpu.VMEM((s, l + HIST, c), F32), pltpu.VMEM((s, 1, c), F32)],
        compiler_params=_params(("parallel", "arbitrary")),
        name="rg_lru",
    )(xb, yb, cbuf, h0, cw, cb, gw, gb, lam)


def _mix_ffn_body(*refs, n_in):
    a_refs, w_refs = refs[:n_in], refs[n_in:2 * n_in]
    x_ref, buf_ref, g1_ref, g2_ref, g3_ref, wup_ref, cw_ref, cb_ref, wdn_ref, y_ref, nbuf_ref, gbuf = refs[2 * n_in:]
    s, l, d = x_ref.shape
    m = s * l
    nw, f = cw_ref.shape
    t = pl.program_id(1)
    mix = None
    for a_ref, w_ref in zip(a_refs, w_refs):
        part = _dot(a_ref[...].reshape(m, a_ref.shape[-1]).astype(BF16), w_ref[...])
        mix = part if mix is None else mix + part
    x = x_ref[...].reshape(m, d) + _rms(mix, g1_ref[...])
    h = _rms(x, g2_ref[...]).astype(BF16)

    @pl.when(t == 0)
    def _():
        gbuf[:, HIST - (nw - 1):HIST, :] = buf_ref[...]

    @pl.when(t > 0)
    def _():
        gbuf[:, 0:HIST, :] = gbuf[:, l:l + HIST, :]

    gbuf[:, HIST:HIST + l, :] = _dot(h, wup_ref[:, :f]).reshape(s, l, f)
    u = _dot(h, wup_ref[:, f:])
    gc = cb_ref[...].reshape(1, 1, f)
    for j in range(nw):
        off = HIST - (nw - 1) + j
        gc = gc + gbuf[:, off:off + l, :] * cw_ref[j:j + 1, :].reshape(1, 1, f)
    act = (_gelu(gc).reshape(m, f) * u).astype(BF16)
    y = _dot(act, wdn_ref[...])
    y_ref[...] = (x + _rms(y, g3_ref[...])).reshape(s, l, d)

    @pl.when(t == pl.num_programs(1) - 1)
    def _():
        nbuf_ref[...] = gbuf[:, l + HIST - (nw - 1):l + HIST, :]


def _mix_ffn(acts, ws, x, buf, g1, g2, g3, wup, cw, cb, wdn, layer, s, l):
    b, t, d = x.shape
    nw, f = cw.shape

    def resident(w):
        if w.ndim == 3:
            return pl.BlockSpec((None,) + w.shape[1:], lambda b_, t_: (layer, 0, 0), pipeline_mode=pl.Buffered(1))
        return pl.BlockSpec(w.shape, lambda b_, t_: (0, 0), pipeline_mode=pl.Buffered(1))

    return pl.pallas_call(
        functools.partial(_mix_ffn_body, n_in=len(acts)),
        grid=(b // s, t // l),
        in_specs=[_tile_spec(s, l, a.shape[-1]) for a in acts] + [resident(w) for w in ws]
                 + [_tile_spec(s, l, d), _seq_spec(s, nw - 1, f), _const_spec((1, d)), _const_spec((1, d)),
                    _const_spec((1, d)), resident(wup), _const_spec(cw.shape), _const_spec((1, f)), resident(wdn)],
        out_specs=[_tile_spec(s, l, d), _seq_spec(s, nw - 1, f)],
        out_shape=[jax.ShapeDtypeStruct((b, t, d), F32), jax.ShapeDtypeStruct((b, nw - 1, f), F32)],
        scratch_shapes=[pltpu.VMEM((s, l + HIST, f), F32)],
        compiler_params=_params(("parallel", "arbitrary")),
        name="mix_ffn",
    )(*acts, *ws, x, buf, g1, g2, g3, wup, cw, cb, wdn)


def _fox_prompt_body(qt_ref, ct_ref, kaug_ref, vaug_ref, o_ref, m_ref, acc_ref, sa_ref, sb_ref, cmax_a_ref, cmax_b_ref,
                     *, tq):
    hp = pl.program_id(1)
    iq = pl.program_id(2)
    qt = qt_ref[0]
    feat = lax.broadcasted_iota(jnp.int32, (LANES, 1), 0)
    qas = []
    for j in range(2):
        head = 2 * hp + j
        hi, mid, lo = _split3(ct_ref[0, pl.ds(head, 1), :])
        pick = jnp.where(((feat & (FOX_HEADS - 1)) == head) & (feat < 3 * FOX_HEADS), 1.0, 0.0)
        augt = jnp.where(feat == 3 * FOX_HEADS, hi,
                         jnp.where(feat == 3 * FOX_HEADS + 1, mid, jnp.where(feat == 3 * FOX_HEADS + 2, lo, pick)))
        own = (feat < FOX_HD) if j == 0 else (feat >= FOX_HD)
        qas.append(jnp.concatenate([jnp.where(own, qt, jnp.zeros_like(qt)), augt.astype(BF16)], axis=0))
    qa = jnp.concatenate(qas, axis=1)
    m_ref[...] = jnp.full_like(m_ref, NEG_INF)
    acc_ref[...] = jnp.zeros_like(acc_ref)
    key = lax.broadcasted_iota(jnp.int32, (tq, tq), 0)
    qry = lax.broadcasted_iota(jnp.int32, (tq, tq), 1)

    def logits(ik, buf):
        s_ref, cmax_ref = buf
        kt = kaug_ref[0, 0, pl.ds(pl.multiple_of(ik * tq, tq), tq), :]
        s2 = _dot(kt, qa)
        s_ref[...] = s2
        cmax_ref[...] = jnp.max(s2, axis=0, keepdims=True)

    def update(ik, buf, masked):
        s_ref, cmax_ref = buf
        for j in range(2):
            st = s_ref[:, j * tq:(j + 1) * tq]
            if masked:
                st = jnp.where(key <= qry, st, NEG_INF)
                tile_max = jnp.max(st, axis=0, keepdims=True)
            else:
                tile_max = cmax_ref[:, j * tq:(j + 1) * tq]
            m_old = m_ref[j]
            m_new = jnp.maximum(m_old, tile_max)
            pt = jnp.exp2(st - m_new).astype(BF16)
            m_ref[j] = m_new
            acc_ref[j] = acc_ref[j] * jnp.exp2(m_old - m_new) + _dot(vaug_ref[0, j, ik], pt)

    buf_a = (sa_ref, cmax_a_ref)
    buf_b = (sb_ref, cmax_b_ref)
    logits(0, buf_a)

    def pair_body(i, carry):
        logits(2 * i + 1, buf_b)
        update(2 * i, buf_a, False)
        logits(2 * i + 2, buf_a)
        update(2 * i + 1, buf_b, False)
        return carry

    lax.fori_loop(0, iq >> 1, pair_body, 0)

    @pl.when((iq & 1) == 0)
    def _():
        update(iq, buf_a, True)

    @pl.when((iq & 1) == 1)
    def _():
        logits(iq, buf_b)
        update(iq - 1, buf_a, False)
        update(iq, buf_b, True)

    a0 = acc_ref[0]
    a1 = acc_ref[1]
    ot = jnp.concatenate([a0[:FOX_HD] / a0[FOX_HD:], a1[FOX_HD:] / a1[:FOX_HD]], axis=0)
    o_ref[0] = ot.T.astype(o_ref.dtype)


def _fox_prompt(qt, ct2, kaug, vaug, tq):
    b, da, t = qt.shape
    nq = t // tq
    assert vaug.shape[2] == nq and vaug.shape[-1] == tq
    return pl.pallas_call(
        functools.partial(_fox_prompt_body, tq=tq),
        grid=(b, FOX_HEADS // 2, nq),
        in_specs=[pl.BlockSpec((1, LANES, tq), lambda b_, h_, i_: (b_, h_, i_)),
                  pl.BlockSpec((1, FOX_HEADS, tq), lambda b_, h_, i_: (b_, 0, i_)),
                  pl.BlockSpec((1, 1) + kaug.shape[2:], lambda b_, h_, i_: (b_, h_, 0, 0)),
                  pl.BlockSpec((1, 2) + vaug.shape[2:], lambda b_, h_, i_: (b_, h_, 0, 0, 0))],
        out_specs=pl.BlockSpec((1, tq, LANES), lambda b_, h_, i_: (b_, i_, h_)),
        out_shape=jax.ShapeDtypeStruct((b, t, da), BF16),
        scratch_shapes=[pltpu.VMEM((2, 1, tq), F32), pltpu.VMEM((2, LANES, tq), F32),
                        pltpu.VMEM((tq, 2 * tq), F32), pltpu.VMEM((tq, 2 * tq), F32),
                        pltpu.VMEM((1, 2 * tq), F32), pltpu.VMEM((1, 2 * tq), F32)],
        compiler_params=_params(("parallel", "parallel", "arbitrary")),
        name="fox_prompt",
    )(qt, ct2, kaug, vaug)


def _fox_sample_body(pt_ref, q_ref, kn_ref, vn_ref, cnt_ref, cnr_ref, *rest, npp):
    kp = rest[0:npp]
    vp = rest[npp:2 * npp]
    lp = rest[2 * npp:3 * npp]
    o_ref, m_ref, l_ref, acc_ref, carry_ref, kb_ref, vb_ref = rest[3 * npp:]
    del pt_ref
    j = pl.program_id(1)
    t, da = q_ref.shape[1], q_ref.shape[2]
    rows = FOX_HEADS * t
    row_c = lax.broadcasted_iota(jnp.int32, (rows, da), 0)
    col_c = lax.broadcasted_iota(jnp.int32, (rows, da), 1)
    own = (col_c & -FOX_HD) == (row_c & -t) * (FOX_HD // t)
    q8 = q_ref[0] * FOX_SCALE
    qbd = jnp.where(own, jnp.concatenate([q8] * FOX_HEADS, axis=0), 0.0).astype(BF16)
    cn_rows = cnr_ref[0]

    def head_rows(x):
        return jnp.concatenate([jnp.broadcast_to(x[h:h + 1, :], (t, x.shape[1])) for h in range(FOX_HEADS)], axis=0)

    @pl.when(j == 0)
    def _():
        pad = jnp.zeros((PAGE_SIZE - t, da), F32)
        kn = jnp.concatenate([kn_ref[0], pad], axis=0).astype(BF16)
        vn = jnp.concatenate([vn_ref[0], pad], axis=0).astype(BF16)
        row_l = lax.broadcasted_iota(jnp.int32, (rows, PAGE_SIZE), 0)
        col_l = lax.broadcasted_iota(jnp.int32, (rows, PAGE_SIZE), 1)
        sc = _dot_nt(qbd, kn) + cn_rows - head_rows(cnt_ref[0])
        sc = jnp.where(col_l <= (row_l & (t - 1)), sc, NEG_INF)
        m0 = jnp.max(sc, axis=-1, keepdims=True)
        p = jnp.exp(sc - m0)
        m_ref[...] = m0
        l_ref[...] = jnp.sum(p, axis=-1, keepdims=True)
        acc_ref[...] = _dot(p.astype(BF16), vn)
        carry_ref[...] = jnp.zeros_like(carry_ref)

    prow = lax.broadcasted_iota(jnp.int32, (PAGE_SIZE, PAGE_SIZE), 0)
    pcol = lax.broadcasted_iota(jnp.int32, (PAGE_SIZE, PAGE_SIZE), 1)
    later = jnp.where(prow > pcol, 1.0, 0.0).astype(F32)
    for i in range(npp):
        kb_ref[:, i * PAGE_SIZE:(i + 1) * PAGE_SIZE] = kp[i][...].astype(BF16)
        vb_ref[:, i * PAGE_SIZE:(i + 1) * PAGE_SIZE] = vp[i][...].astype(BF16)
    lf_all = jnp.concatenate([lp[i][...] for i in range(npp)], axis=0)
    inside = _dot(lf_all, later, HIGHEST)
    biases = []
    carry = carry_ref[...]
    for i in range(npp):
        rows_i = slice(i * FOX_HEADS, (i + 1) * FOX_HEADS)
        biases.append(head_rows(inside[rows_i] + carry))
        carry = carry + (inside[rows_i, 0:1] + lf_all[rows_i, 0:1])
    carry_ref[...] = carry
    sc = _dot(qbd, kb_ref[...]) + cn_rows + jnp.concatenate(biases, axis=1)
    m_old = m_ref[...]
    m_new = jnp.maximum(m_old, jnp.max(sc, axis=-1, keepdims=True))
    alpha = jnp.exp(m_old - m_new)
    p = jnp.exp(sc - m_new)
    l_new = alpha * l_ref[...] + jnp.sum(p, axis=-1, keepdims=True)
    acc = acc_ref[...] * alpha + _dot_nt(p.astype(BF16), vb_ref[...])
    m_ref[...] = m_new
    l_ref[...] = l_new
    acc_ref[...] = acc

    @pl.when(j == pl.num_programs(1) - 1)
    def _():
        full = jnp.where(own, acc / l_new, 0.0)
        out = full[0:t, :]
        for h in range(1, FOX_HEADS):
            out = out + full[h * t:(h + 1) * t, :]
        o_ref[0] = out


def _fox_sample(q, k, v, c, cache_k, cache_v, cache_logf, page_table, layer):
    b, t, da = q.shape
    n_pages = page_table.shape[1]
    npp = PAGES_PER_STEP
    assert n_pages % npp == 0 and PAGE_SIZE % t == 0 and t % SUBLANES == 0
    n_pool, n_layers = cache_k.shape[0], cache_k.shape[1]
    ckt = cache_k.transpose(0, 1, 3, 4, 2).reshape(n_pool, n_layers, da, PAGE_SIZE)
    cvt = cache_v.transpose(0, 1, 3, 4, 2).reshape(n_pool, n_layers, da, PAGE_SIZE)
    clt = cache_logf.transpose(0, 1, 3, 2)
    ct = c.transpose(0, 2, 1)
    cn_t = jnp.pad(ct, ((0, 0), (0, 0), (0, PAGE_SIZE - t)))
    cn_rows = ct.reshape(b, FOX_HEADS * t, 1)

    def page_spec(i, height):
        return pl.BlockSpec((None, None, height, PAGE_SIZE),
                            lambda b_, j_, pt: (pt[b_, n_pages - 1 - (j_ * npp + i)], layer, 0, 0))

    tok = pl.BlockSpec((1, t, da), lambda b_, j_, pt: (b_, 0, 0))
    rows = FOX_HEADS * t
    grid_spec = pltpu.PrefetchScalarGridSpec(
        num_scalar_prefetch=1,
        grid=(b, n_pages // npp),
        in_specs=[tok, tok, tok,
                  pl.BlockSpec((1, FOX_HEADS, PAGE_SIZE), lambda b_, j_, pt: (b_, 0, 0)),
                  pl.BlockSpec((1, rows, 1), lambda b_, j_, pt: (b_, 0, 0))]
                 + [page_spec(i, da) for i in range(npp)] * 2
                 + [page_spec(i, FOX_HEADS) for i in range(npp)],
        out_specs=tok,
        scratch_shapes=[pltpu.VMEM((rows, 1), F32), pltpu.VMEM((rows, 1), F32), pltpu.VMEM((rows, da), F32),
                        pltpu.VMEM((FOX_HEADS, 1), F32),
                        pltpu.VMEM((da, npp * PAGE_SIZE), BF16), pltpu.VMEM((da, npp * PAGE_SIZE), BF16)],
    )
    return pl.pallas_call(
        functools.partial(_fox_sample_body, npp=npp),
        grid_spec=grid_spec,
        out_shape=jax.ShapeDtypeStruct((b, t, da), F32),
        compiler_params=_params(("parallel", "arbitrary")),
        name="fox_sample",
    )(page_table, q, k, v, cn_t, cn_rows, *([ckt] * npp), *([cvt] * npp), *([clt] * npp))


def _pow2_tile(t, want):
    l = min(t, want)
    assert t % l == 0 and l & (l - 1) == 0
    return l


def _prep_weights(p):
    w = {}
    n_a = p['w_in_a'].shape[0]
    wa = p['w_in_a']
    d = wa.shape[1]
    q_end = 2 * GLA_HEADS * GLA_DK + GLA_HEADS * GLA_DV
    rank = p['w_gk2'].shape[1]
    w['w_in_a'] = jnp.concatenate(
        [wa[:, :, :q_end], wa[:, :, q_end + rank:], wa[:, :, q_end:q_end + rank],
         jnp.zeros((n_a, d, LANES - rank), wa.dtype)], axis=-1).astype(BF16)
    w['w_gk2'] = jnp.pad(p['w_gk2'], ((0, 0), (0, LANES - rank), (0, 0)))
    gw = p['lru_gate_w']
    nb, bw = gw.shape[2], gw.shape[3]
    eye = jnp.eye(nb, dtype=gw.dtype)
    dense = jnp.einsum('agncd,nm->agncmd', gw, eye).reshape(n_a, 2, nb * bw, nb * bw)
    w['lru_gate_w'] = jnp.concatenate([dense[:, 0], dense[:, 1]], axis=-1).astype(BF16)
    w['lru_gate_b'] = p['lru_gate_b'].reshape(n_a, 1, -1)
    w['w_out_a'] = p['w_out_a'].astype(BF16)
    wc = p['w_in_c']
    n_c = wc.shape[0]
    da = FOX_HEADS * FOX_HD
    w['w_in_c'] = jnp.concatenate([wc, jnp.zeros((n_c, d, LANES - FOX_HEADS), wc.dtype)], axis=-1).astype(BF16)
    w['b_f'] = jnp.pad(p['b_f'], ((0, 0), (0, LANES - FOX_HEADS))).reshape(n_c, 1, LANES)
    w['w_qkvf_t'] = jnp.swapaxes(w['w_in_c'], 1, 2)
    w['b_f_col'] = w['b_f'].reshape(n_c, LANES, 1)
    w['eye'] = jnp.eye(LANES, dtype=BF16)
    w['w_out_c'] = p['w_out_c'].astype(BF16)
    w['ffn_w_up'] = p['ffn_w_up'].astype(BF16)
    w['ffn_w_down'] = p['ffn_w_down'].astype(BF16)
    return w


def _trunk(x, gla_s, lru_buf, lru_h, ffn_buf, layer_c, p, w, s, tiles):
    b, t, d = x.shape
    depth = p['norm_g'].shape[0]
    l_proj, l_gla, l_lru, l_ffn = (_pow2_tile(t, n) for n in tiles)
    chunk = np.gcd(t, GLA_CHUNK).item()
    a_widths = [GLA_HEADS * GLA_DK] * 2 + [GLA_HEADS * GLA_DV] * 2 + [lru_buf.shape[-1]] * 2 + [LANES]
    w_split = GLA_HEADS * GLA_DV
    gla_l, buf_l, h_l, k_l, v_l, lf_l, ffn_l = [], [], [], [], [], [], []
    ia = ic = 0
    for layer in range(depth):
        ng = p['norm_g'][layer].reshape(4, 1, d)
        if layer % 2 == 0:
            q, k, v, gg, xb, yb, lo = _norm_proj(x, ng[0], w['w_in_a'][ia], a_widths, s, l_proj)
            s0t = gla_s[:, ia].reshape(b, GLA_HEADS * GLA_DK, GLA_DV).transpose(0, 2, 1)
            o, snt = _gla(q, k, v, lo, gg, w['w_gk2'][ia], p['b_gk2'][ia].reshape(1, -1),
                          p['gla_norm_g'][ia].reshape(1, -1), s0t, s if s > 1 else min(b, GLA_SEQS), l_gla, chunk)
            lru_o, nbuf, nh = _lru(xb, yb, lru_buf[:, ia], lru_h[:, ia][:, None, :], p['lru_conv_w'][ia],
                                   p['lru_conv_b'][ia].reshape(1, -1), w['lru_gate_w'][ia], w['lru_gate_b'][ia],
                                   p['lru_lambda'][ia].reshape(1, -1), s, l_lru)
            acts, ws = [o, lru_o], [w['w_out_a'][ia][:w_split], w['w_out_a'][ia][w_split:]]
            gla_l.append(snt.transpose(0, 2, 1).reshape(b, GLA_HEADS, GLA_DK, GLA_DV))
            buf_l.append(nbuf)
            h_l.append(nh[:, 0, :])
            ia += 1
        else:
            o, k_new, v_new, lf_new = layer_c(x, ng[0], ic)
            acts, ws = [o], [w['w_out_c'][ic]]
            k_l.append(k_new)
            v_l.append(v_new)
            lf_l.append(lf_new)
            ic += 1
        x, fb = _mix_ffn(acts, ws, x, ffn_buf[:, layer], ng[1], ng[2], ng[3], w['ffn_w_up'],
                         p['ffn_conv_w'][layer], p['ffn_conv_b'][layer].reshape(1, -1), w['ffn_w_down'], layer, s, l_ffn)
        ffn_l.append(fb)
    return (x, jnp.stack(gla_l, axis=1), jnp.stack(buf_l, axis=1), jnp.stack(h_l, axis=1), jnp.stack(k_l, axis=1),
            jnp.stack(v_l, axis=1), jnp.stack(lf_l, axis=1), jnp.stack(ffn_l, axis=1))


PROMPT_TILES = (512, 512, 256, 512)
FOX_TQ = 512


def kernel(x_prompt, x_sample, state_gla, state_lru_conv, state_lru_h, cache_k, cache_v, cache_logf, state_ffn_conv, page_table, norm_g, w_in_a, w_gk2, b_gk2, gla_norm_g, lru_conv_w, lru_conv_b, lru_gate_w, lru_gate_b, lru_lambda, w_out_a, w_in_c, b_f, w_out_c, ffn_w_up, ffn_conv_w, ffn_conv_b, ffn_w_down):
    p = {'norm_g': norm_g, 'w_in_a': w_in_a, 'w_gk2': w_gk2, 'b_gk2': b_gk2, 'gla_norm_g': gla_norm_g,
         'lru_conv_w': lru_conv_w, 'lru_conv_b': lru_conv_b, 'lru_gate_w': lru_gate_w, 'lru_gate_b': lru_gate_b,
         'lru_lambda': lru_lambda, 'w_out_a': w_out_a, 'w_in_c': w_in_c, 'b_f': b_f, 'w_out_c': w_out_c,
         'ffn_w_up': ffn_w_up, 'ffn_conv_w': ffn_conv_w, 'ffn_conv_b': ffn_conv_b, 'ffn_w_down': ffn_w_down}
    w = _prep_weights(p)
    bp = x_prompt.shape[0]
    bs, ts, _ = x_sample.shape
    dt = x_prompt.dtype
    gla0 = jnp.zeros((bp,) + state_gla.shape[1:], dt)
    lru_buf0 = jnp.zeros((bp,) + state_lru_conv.shape[1:], dt)
    lru_h0 = jnp.zeros((bp,) + state_lru_h.shape[1:], dt)
    ffn_buf0 = jnp.zeros((bp,) + state_ffn_conv.shape[1:], dt)

    def layer_c_prompt(x, g, ic):
        b, t, _ = x.shape
        l = _pow2_tile(t, FOX_TQ)
        qt, kt, vt, lft, ct2, kaug, vaug = _proj_c_prompt(x, g, w['w_qkvf_t'][ic], w['b_f_col'][ic], w['eye'], l)
        o = _fox_prompt(qt, ct2, kaug, vaug, l)
        k_new = kt.reshape(b, FOX_HEADS, FOX_HD, t).transpose(0, 3, 1, 2)
        v_new = vt.reshape(b, FOX_HEADS, FOX_HD, t).transpose(0, 3, 1, 2)
        return o, k_new, v_new, lft.transpose(0, 2, 1)

    def layer_c_sample(x, g, ic):
        b, t, _ = x.shape
        q, k, v, lf, c = _proj_c_sample(x, g, w['w_in_c'][ic], w['b_f'][ic])
        o = _fox_sample(q, k, v, c, cache_k, cache_v, cache_logf, page_table, ic)
        return o, k.reshape(b, t, FOX_HEADS, FOX_HD), v.reshape(b, t, FOX_HEADS, FOX_HD), lf

    outs_p = _trunk(x_prompt, gla0, lru_buf0, lru_h0, ffn_buf0, layer_c_prompt, p, w, 1, PROMPT_TILES)
    outs_s = _trunk(x_sample, state_gla, state_lru_conv, state_lru_h, state_ffn_conv, layer_c_sample, p, w, bs,
                    (ts, ts, ts, ts))
    return tuple(o for pair in zip(outs_p, outs_s) for o in pair)
```
